```python
import math
import jax, jax.numpy as jnp
from jax import lax
import numpy as np

D_MODEL = 4096
BATCH = 2
SEQ = 8192
DEPTH = 4

HEAD_DIM_A = 64
N_Q_A = 16
N_KV_A = 2
WINDOW = 128
SSM_WIDTH = 1024
SSM_GROUP = 16
SSM_GROUPS = SSM_WIDTH // SSM_GROUP
SSM_STATE = 64
N_SB = 8
HEAD_DIM_SB = 128
SB_BLOCK = 128
BRANCH_WIDTH = 1024
N_BRANCH = 3
D_FF = 7168
CONV_W = 3
N_MOD = 6
EPS = 1e-6
NEG_INF = -1e30

Q_A = N_Q_A * HEAD_DIM_A
KV_A = N_KV_A * HEAD_DIM_A
SB_W = N_SB * HEAD_DIM_SB
IN_SPLITS = [Q_A, KV_A, KV_A, SSM_WIDTH, SB_W, SB_W, SB_W, N_BRANCH * D_MODEL]
N_IN = sum(IN_SPLITS)

kernel_name = "hybrid_gated_swa_s5_stickbreak_block"


def rmsnorm(x, g):
    xf = x.astype(jnp.float32)
    y = xf * lax.rsqrt(jnp.mean(xf * xf, axis=-1, keepdims=True) + EPS)
    return (y * g.astype(jnp.float32)).astype(x.dtype)


def alibi_slopes(n):
    return jnp.exp2(-8.0 * jnp.arange(1, n + 1, dtype=jnp.float32) / n)


def sliding_window_attention(q, k, v, sinks):
    B, S, _ = q.shape
    nb = S // WINDOW
    G = N_Q_A // N_KV_A
    qb = q.reshape(B, nb, WINDOW, N_KV_A, G, HEAD_DIM_A)
    kb = k.reshape(B, nb, WINDOW, N_KV_A, HEAD_DIM_A)
    vb = v.reshape(B, nb, WINDOW, N_KV_A, HEAD_DIM_A)

    def with_prev(t):
        prev = jnp.pad(t, ((0, 0), (1, 0), (0, 0), (0, 0), (0, 0)))[:, :-1]
        return jnp.concatenate([prev, t], axis=2)

    kk, vv = with_prev(kb), with_prev(vb)
    s = jnp.einsum('bnqhgd,bnkhd->bnhgqk', qb, kk,
                   preferred_element_type=jnp.float32) * (HEAD_DIM_A ** -0.5)
    q_pos = jnp.arange(nb)[:, None] * WINDOW + jnp.arange(WINDOW)[None, :]
    k_pos = jnp.arange(nb)[:, None] * WINDOW - WINDOW + jnp.arange(2 * WINDOW)[None, :]
    dist = q_pos[:, :, None] - k_pos[:, None, :]
    valid = (dist >= 0) & (dist < WINDOW) & (k_pos[:, None, :] >= 0)
    slopes = alibi_slopes(N_Q_A).reshape(N_KV_A, G)
    bias = -slopes[None, :, :, None, None] * dist[:, None, None, :, :].astype(jnp.float32)
    s = jnp.where(valid[:, None, None], s + bias, NEG_INF)
    sink = jnp.broadcast_to(sinks.astype(jnp.float32).reshape(N_KV_A, G)[None, None, :, :, None, None],
                            s.shape[:-1] + (1,))
    p = jax.nn.softmax(jnp.concatenate([s, sink], axis=-1), axis=-1)[..., :-1]
    o = jnp.einsum('bnhgqk,bnkhd->bnqhgd', p.astype(v.dtype), vv)
    return o.reshape(B, S, Q_A)


def s5_ssm(u, a_re, a_im, log_dt, b_re, b_im, c_re, c_im, d_skip, w_glu):
    B, S, _ = u.shape
    uf = u.astype(jnp.float32).reshape(B, S, SSM_GROUPS, SSM_GROUP)
    lam = lax.complex(a_re.astype(jnp.float32), a_im.astype(jnp.float32))
    dt = jnp.exp(log_dt.astype(jnp.float32))[:, None]
    a_bar = jnp.exp(lam * dt)
    b = lax.complex(b_re.astype(jnp.float32), b_im.astype(jnp.float32))
    b_bar = ((a_bar - 1.0) / lam)[..., None] * b
    bu = jnp.einsum('bsgp,gnp->bsgn', uf.astype(jnp.complex64), b_bar)
    a_seq = jnp.broadcast_to(a_bar, bu.shape)

    def combine(left, right):
        a_l, b_l = left
        a_r, b_r = right
        return a_r * a_l, a_r * b_l + b_r

    _, states = lax.associative_scan(combine, (a_seq, bu), axis=1)
    cm = lax.complex(c_re.astype(jnp.float32), c_im.astype(jnp.float32))
    y = jnp.einsum('bsgn,gpn->bsgp', states, cm).real \
        + d_skip.astype(jnp.float32).reshape(SSM_GROUPS, SSM_GROUP) * uf
    y = jax.nn.gelu(y.reshape(B, S, SSM_WIDTH)).astype(u.dtype)
    val, gate = jnp.split(y @ w_glu, 2, axis=-1)
    return val * jax.nn.sigmoid(gate)


def stick_breaking_attention(q, k, v):
    B, S, _ = q.shape
    nb = S // SB_BLOCK
    qb = q.reshape(B, nb, SB_BLOCK, N_SB, HEAD_DIM_SB).transpose(1, 0, 2, 3, 4)
    k = k.reshape(B, S, N_SB, HEAD_DIM_SB)
    v = v.reshape(B, S, N_SB, HEAD_DIM_SB)
    k_pos = jnp.arange(S)
    scale = HEAD_DIM_SB ** -0.5

    def block(args):
        qblk, i = args
        z = jnp.einsum('bqhd,bkhd->bhqk', qblk, k, preferred_element_type=jnp.float32) * scale
        q_pos = i * SB_BLOCK + jnp.arange(SB_BLOCK)
        causal = k_pos[None, :] < q_pos[:, None]
        log_keep = jnp.where(causal, jax.nn.log_sigmoid(-z), 0.0)
        between = lax.cumsum(log_keep, axis=3, reverse=True) - log_keep
        w = jnp.where(causal, jnp.exp(jax.nn.log_sigmoid(z) + between), 0.0)
        return jnp.einsum('bhqk,bkhd->bqhd', w.astype(v.dtype), v)

    o = lax.map(block, (qb, jnp.arange(nb)))
    return o.transpose(1, 0, 2, 3, 4).reshape(B, S, SB_W)


def conv_gated_mlp(h, w_up, conv_w, conv_b, w_down):
    S = h.shape[1]
    u = h @ w_up
    up = jnp.pad(u, ((0, 0), (CONV_W - 1, 0), (0, 0)))
    acc = conv_b
    for tap in range(CONV_W):
        acc = acc + conv_w[tap] * up[:, tap:tap + S]
    a, b = jnp.split(acc, 2, axis=-1)
    return (jax.nn.gelu(a) * b) @ w_down


def setup_inputs(seed: int = 0) -> dict:
    key = jax.random.key(seed)
    ks = jax.random.split(key, 32)
    f32 = jnp.float32

    def nrm(k, shape, scale):
        return jax.random.normal(k, shape, f32) * scale

    L, D, F = DEPTH, D_MODEL, D_FF
    G, N, P = SSM_GROUPS, SSM_STATE, SSM_GROUP
    return {
        "x": nrm(ks[0], (BATCH, SEQ, D), 1.0),
        "c": nrm(ks[1], (BATCH, D), 1.0),
        "w_ada": nrm(ks[2], (D, N_MOD * D), 0.5 * D ** -0.5),
        "b_ada": nrm(ks[3], (N_MOD * D,), 0.02),
        "ada_table": nrm(ks[4], (L, N_MOD, D), 0.1),
        "norm_mix_pre": 1.0 + nrm(ks[5], (L, D), 0.05),
        "norm_mix_post": 1.0 + nrm(ks[6], (L, D), 0.05),
        "norm_ffn_pre": 1.0 + nrm(ks[7], (L, D), 0.05),
        "norm_ffn_post": 1.0 + nrm(ks[8], (L, D), 0.05),
        "w_in": nrm(ks[9], (L, D, N_IN), D ** -0.5),
        "attn_sinks": nrm(ks[10], (L, N_Q_A), 0.5),
        "ssm_a_re": -0.5 * jnp.exp(nrm(ks[11], (L, G, N), 0.05)),
        "ssm_a_im": math.pi * jnp.arange(N, dtype=f32)[None, None, :] + nrm(ks[12], (L, G, N), 0.05),
        "ssm_log_dt": jax.random.uniform(ks[13], (L, G), f32, math.log(0.001), math.log(0.1)),
        "ssm_b_re": nrm(ks[14], (L, G, N, P), (2 * P) ** -0.5),
        "ssm_b_im": nrm(ks[15], (L, G, N, P), (2 * P) ** -0.5),
        "ssm_c_re": nrm(ks[16], (L, G, P, N), (2 * N) ** -0.5),
        "ssm_c_im": nrm(ks[17], (L, G, P, N), (2 * N) ** -0.5),
        "ssm_d": nrm(ks[18], (L, SSM_WIDTH), 1.0),
        "ssm_w_glu": nrm(ks[19], (L, SSM_WIDTH, 2 * SSM_WIDTH), SSM_WIDTH ** -0.5),
        "w_branch": nrm(ks[20], (L, N_BRANCH, BRANCH_WIDTH, D), BRANCH_WIDTH ** -0.5),
        "w_out": nrm(ks[21], (L, D, D), D ** -0.5),
        "ffn_w_up": nrm(ks[22], (L, D, 2 * F), D ** -0.5),
        "ffn_conv_w": nrm(ks[23], (L, CONV_W, 2 * F), CONV_W ** -0.5),
        "ffn_conv_b": nrm(ks[24], (L, 2 * F), 0.02),
        "ffn_w_down": nrm(ks[25], (L, F, D), F ** -0.5),
    }


def reference(x, c, w_ada, b_ada, ada_table, norm_mix_pre, norm_mix_post, norm_ffn_pre, norm_ffn_post,
              w_in, attn_sinks, ssm_a_re, ssm_a_im, ssm_log_dt, ssm_b_re, ssm_b_im, ssm_c_re, ssm_c_im,
              ssm_d, ssm_w_glu, w_branch, w_out, ffn_w_up, ffn_conv_w, ffn_conv_b, ffn_w_down):
    B, S, D = x.shape
    mod_shared = (jax.nn.silu(c) @ w_ada + b_ada).reshape(B, N_MOD, D)
    split_points = np.cumsum(IN_SPLITS)[:-1].tolist()
    for l in range(DEPTH):
        mod = mod_shared + ada_table[l][None]
        shift_m, scale_m, gate_m, shift_f, scale_f, gate_f = [mod[:, i, None, :] for i in range(N_MOD)]

        h = rmsnorm(x, norm_mix_pre[l]) * (1.0 + scale_m) + shift_m
        proj = h @ w_in[l]
        q_a, k_a, v_a, u_s, q_s, k_s, v_s, g = jnp.split(proj, split_points, axis=-1)
        y_a = sliding_window_attention(q_a, k_a, v_a, attn_sinks[l])
        y_s = s5_ssm(u_s, ssm_a_re[l], ssm_a_im[l], ssm_log_dt[l], ssm_b_re[l], ssm_b_im[l],
                     ssm_c_re[l], ssm_c_im[l], ssm_d[l], ssm_w_glu[l])
        y_c = stick_breaking_attention(q_s, k_s, v_s)
        branches = jnp.stack([y_a, y_s, y_c], axis=2)
        gates = jax.nn.sigmoid(g.reshape(B, S, N_BRANCH, D))
        merged = jnp.sum(gates * jnp.einsum('bskc,kcd->bskd', branches, w_branch[l]), axis=2)
        y = merged @ w_out[l]
        x = x + gate_m * rmsnorm(y, norm_mix_post[l])

        h = rmsnorm(x, norm_ffn_pre[l]) * (1.0 + scale_f) + shift_f
        y = conv_gated_mlp(h, ffn_w_up[l], ffn_conv_w[l], ffn_conv_b[l], ffn_w_down[l])
        x = x + gate_f * rmsnorm(y, norm_ffn_post[l])
    return x
```

```python
import functools
import math

import jax
import jax.numpy as jnp
from jax import lax
from jax.experimental import pallas as pl
from jax.experimental.pallas import tpu as pltpu

HEAD_DIM_A = 64
N_Q_A = 16
N_KV_A = 2
WINDOW = 128
SSM_WIDTH = 1024
SSM_GROUP = 16
SSM_GROUPS = SSM_WIDTH // SSM_GROUP
SSM_STATE = 64
N_SB = 8
HEAD_DIM_SB = 128
BRANCH_WIDTH = 1024
N_BRANCH = 3
CONV_W = 3
N_MOD = 6
EPS = 1e-6
NEG_INF = -1e30

Q_A = N_Q_A * HEAD_DIM_A
KV_A = N_KV_A * HEAD_DIM_A
SB_W = N_SB * HEAD_DIM_SB

OFF_QA = 0
OFF_US = OFF_QA + Q_A
OFF_QS = OFF_US + SSM_WIDTH
OFF_KS = OFF_QS + SB_W
OFF_VS = OFF_KS + SB_W
OFF_G = OFF_VS + SB_W

LANES = 128
BF16_SUBLANES = 16
VMEM_LIMIT_BYTES = 56 * 1024 * 1024

SSM_GROUPS_PER_BLOCK = LANES // SSM_GROUP
SSM_BLOCKS = SSM_GROUPS // SSM_GROUPS_PER_BLOCK
SSM_STATES_PER_BLOCK = SSM_GROUPS_PER_BLOCK * SSM_STATE
SSM_STATES = SSM_GROUPS * SSM_STATE
SSM_SCAN_COLS = 1024
SSM_SCAN_TILES = SSM_SCAN_COLS // LANES
SSM_STATE_TILES = SSM_STATES // LANES


def _tile(dim, pref, align):
    t = (min(pref, dim) // align) * align
    while t >= align:
        if dim % t == 0:
            return t
        t -= align
    return dim


def _params(*sem):
    return pltpu.CompilerParams(dimension_semantics=sem, vmem_limit_bytes=VMEM_LIMIT_BYTES)


def _gelu_tanh(x):
    return 0.5 * x * (1.0 + jnp.tanh(math.sqrt(2.0 / math.pi) * (x + 0.044715 * (x * x * x))))


def _sigmoid(x):
    return 1.0 / (1.0 + jnp.exp(-x))


def _ada_kernel(c_ref, w_ref, b_ref, o_ref):
    c = c_ref[...]
    a = (c * _sigmoid(c)).astype(jnp.bfloat16)
    o_ref[...] = jnp.dot(a, w_ref[...].astype(jnp.bfloat16),
                         preferred_element_type=jnp.float32) + b_ref[...]


def _ada_proj(c, w_ada, b_ada):
    B, D = c.shape
    N = w_ada.shape[1]
    rows = 8
    c_pad = jnp.zeros((rows, D), jnp.float32).at[:B].set(c)
    tn = _tile(N, 512, LANES)
    out = pl.pallas_call(
        _ada_kernel,
        grid=(N // tn,),
        in_specs=[pl.BlockSpec((rows, D), lambda n: (0, 0)),
                  pl.BlockSpec((D, tn), lambda n: (0, n)),
                  pl.BlockSpec((1, tn), lambda n: (0, n))],
        out_specs=pl.BlockSpec((rows, tn), lambda n: (0, n)),
        out_shape=jax.ShapeDtypeStruct((rows, N), jnp.float32),
        compiler_params=_params("parallel"),
        name="ada_proj",
    )(c_pad, w_ada, b_ada.reshape(1, N))
    return out[:B].reshape(B, N_MOD, D)


def _normmod_kernel(x_ref, g_ref, mod_ref, tab_ref, h_ref, *, shift_idx):
    x = x_ref[0]
    ms = jnp.mean(x * x, axis=-1, keepdims=True)
    y = x * lax.rsqrt(ms + EPS) * g_ref[...]
    shift = mod_ref[0, shift_idx:shift_idx + 1, :] + tab_ref[shift_idx:shift_idx + 1, :]
    scale = mod_ref[0, shift_idx + 1:shift_idx + 2, :] + tab_ref[shift_idx + 1:shift_idx + 2, :]
    h_ref[0] = (y * (1.0 + scale) + shift).astype(h_ref.dtype)


def _normmod(x, g, mod_shared, tab, shift_idx):
    B, S, D = x.shape
    tr = _tile(S, 256, BF16_SUBLANES)
    return pl.pallas_call(
        functools.partial(_normmod_kernel, shift_idx=shift_idx),
        grid=(B, S // tr),
        in_specs=[pl.BlockSpec((1, tr, D), lambda b, i: (b, i, 0)),
                  pl.BlockSpec((1, D), lambda b, i: (0, 0)),
                  pl.BlockSpec((1, N_MOD, D), lambda b, i: (b, 0, 0)),
                  pl.BlockSpec((N_MOD, D), lambda b, i: (0, 0))],
        out_specs=pl.BlockSpec((1, tr, D), lambda b, i: (b, i, 0)),
        out_shape=jax.ShapeDtypeStruct((B, S, D), jnp.bfloat16),
        compiler_params=_params("parallel", "parallel"),
        name="normmod",
    )(x, g.reshape(1, D), mod_shared, tab)


def _postnorm_kernel(x_ref, y_ref, g_ref, mod_ref, tab_ref, o_ref, *, gate_idx):
    y = y_ref[0].astype(jnp.float32)
    ms = jnp.mean(y * y, axis=-1, keepdims=True)
    yn = y * lax.rsqrt(ms + EPS) * g_ref[...]
    gate = mod_ref[0, gate_idx:gate_idx + 1, :] + tab_ref[gate_idx:gate_idx + 1, :]
    o_ref[0] = x_ref[0] + gate * yn


def _postnorm_residual(x, y, g, mod_shared, tab, gate_idx):
    B, S, D = x.shape
    tr = _tile(S, 256, 8)
    return pl.pallas_call(
        functools.partial(_postnorm_kernel, gate_idx=gate_idx),
        grid=(B, S // tr),
        in_specs=[pl.BlockSpec((1, tr, D), lambda b, i: (b, i, 0)),
                  pl.BlockSpec((1, tr, D), lambda b, i: (b, i, 0)),
                  pl.BlockSpec((1, D), lambda b, i: (0, 0)),
                  pl.BlockSpec((1, N_MOD, D), lambda b, i: (b, 0, 0)),
                  pl.BlockSpec((N_MOD, D), lambda b, i: (0, 0))],
        out_specs=pl.BlockSpec((1, tr, D), lambda b, i: (b, i, 0)),
        out_shape=jax.ShapeDtypeStruct((B, S, D), jnp.float32),
        compiler_params=_params("parallel", "parallel"),
        name="postnorm_residual",
    )(x, y.reshape(B, S, D), g.reshape(1, D), mod_shared, tab)


def _mm_kernel(a_ref, w_ref, o_ref):
    o_ref[...] = jnp.dot(a_ref[...], w_ref[...],
                         preferred_element_type=jnp.float32).astype(o_ref.dtype)


def _matmul(a, w, out_dtype, tm_pref, tn_pref, name):
    M, K = a.shape
    N = w.shape[1]
    tm = _tile(M, tm_pref, BF16_SUBLANES)
    tn = _tile(N, tn_pref, LANES)
    return pl.pallas_call(
        _mm_kernel,
        grid=(M // tm, N // tn),
        in_specs=[pl.BlockSpec((tm, K), lambda m, n: (m, 0)),
                  pl.BlockSpec((K, tn), lambda m, n: (0, n))],
        out_specs=pl.BlockSpec((tm, tn), lambda m, n: (m, n)),
        out_shape=jax.ShapeDtypeStruct((M, N), out_dtype),
        compiler_params=_params("parallel", "arbitrary"),
        name=name,
    )(a, w)


def _swa_kernel(sink_ref, q_ref, kp_ref, kc_ref, vp_ref, vc_ref, o_ref):
    i = pl.program_id(1)
    W = WINDOW
    G = N_Q_A // N_KV_A
    r = lax.broadcasted_iota(jnp.int32, (W, W), 0)
    c = lax.broadcasted_iota(jnp.int32, (W, W), 1)
    dist_p = (r + W - c).astype(jnp.float32)
    valid_p = jnp.logical_and(c > r, i > 0)
    dist_c = (r - c).astype(jnp.float32)
    valid_c = c <= r
    dn = (((1,), (1,)), ((), ()))
    for hq in range(N_Q_A):
        hk = hq // G
        slope = 2.0 ** (-8.0 * (hq + 1) / N_Q_A)
        sink = sink_ref[hq]
        q = q_ref[:, hq * HEAD_DIM_A:(hq + 1) * HEAD_DIM_A]
        ksl = slice(hk * HEAD_DIM_A, (hk + 1) * HEAD_DIM_A)
        sp = lax.dot_general(q, kp_ref[:, ksl], dn, preferred_element_type=jnp.float32)
        sc = lax.dot_general(q, kc_ref[:, ksl], dn, preferred_element_type=jnp.float32)
        sp = jnp.where(valid_p, sp * (HEAD_DIM_A ** -0.5) - slope * dist_p, NEG_INF)
        sc = jnp.where(valid_c, sc * (HEAD_DIM_A ** -0.5) - slope * dist_c, NEG_INF)
        m = jnp.maximum(jnp.maximum(jnp.max(sp, axis=-1, keepdims=True),
                                    jnp.max(sc, axis=-1, keepdims=True)), sink)
        ep = jnp.exp(sp - m)
        ec = jnp.exp(sc - m)
        denom = (jnp.sum(ep, axis=-1, keepdims=True) + jnp.sum(ec, axis=-1, keepdims=True)
                 + jnp.exp(sink - m))
        inv = 1.0 / denom
        o = (jnp.dot((ep * inv).astype(jnp.bfloat16), vp_ref[:, ksl],
                     preferred_element_type=jnp.float32)
             + jnp.dot((ec * inv).astype(jnp.bfloat16), vc_ref[:, ksl],
                       preferred_element_type=jnp.float32))
        o_ref[:, hq * HEAD_DIM_A:(hq + 1) * HEAD_DIM_A] = o.astype(o_ref.dtype)


def _swa(proj, sinks, B, S, off_ka, off_va):
    nb = S // WINDOW
    kb = off_ka // KV_A
    vb = off_va // KV_A
    cur = lambda col: (lambda b, i: (b * nb + i, col))
    prev = lambda col: (lambda b, i: (b * nb + jnp.maximum(i - 1, 0), col))
    return pl.pallas_call(
        _swa_kernel,
        grid=(B, nb),
        in_specs=[pl.BlockSpec(memory_space=pltpu.SMEM),
                  pl.BlockSpec((WINDOW, Q_A), cur(OFF_QA // Q_A)),
                  pl.BlockSpec((WINDOW, KV_A), prev(kb)),
                  pl.BlockSpec((WINDOW, KV_A), cur(kb)),
                  pl.BlockSpec((WINDOW, KV_A), prev(vb)),
                  pl.BlockSpec((WINDOW, KV_A), cur(vb))],
        out_specs=pl.BlockSpec((WINDOW, Q_A), lambda b, i: (b * nb + i, 0)),
        out_shape=jax.ShapeDtypeStruct((B * S, Q_A), jnp.bfloat16),
        compiler_params=_params("parallel", "parallel"),
        name="swa",
    )(sinks, proj, proj, proj, proj, proj)


def _ssm_kernel(u_ref, wbr_ref, wbi_ref, ar_ref, ai_ref, cr_ref, ci_ref, d_ref, wg_ref,
                o_ref, sr_ref, si_ref, xr_ref, xi_ref, *, B, Lc):
    @pl.when(pl.program_id(0) == 0)
    def _():
        xr_ref[...] = jnp.zeros_like(xr_ref)
        xi_ref[...] = jnp.zeros_like(xi_ref)

    u = u_ref[...].reshape(B * Lc, SSM_WIDTH)
    lpb = SSM_STATES_PER_BLOCK // LANES
    for j in range(SSM_BLOCKS):
        uj = u[:, j * LANES:(j + 1) * LANES]
        br = jnp.dot(uj, wbr_ref[j], preferred_element_type=jnp.float32)
        bi = jnp.dot(uj, wbi_ref[j], preferred_element_type=jnp.float32)
        for k in range(lpb):
            sr_ref[j * lpb + k] = br[:, k * LANES:(k + 1) * LANES]
            si_ref[j * lpb + k] = bi[:, k * LANES:(k + 1) * LANES]

    for cc in range(SSM_STATES // SSM_SCAN_COLS):
        tiles = pl.ds(cc * SSM_SCAN_TILES, SSM_SCAN_TILES)
        ar = jnp.broadcast_to(ar_ref[tiles], (SSM_SCAN_TILES, B, LANES))
        ai = jnp.broadcast_to(ai_ref[tiles], (SSM_SCAN_TILES, B, LANES))

        def step(t, carry):
            xr, xi = carry
            rows = pl.ds(t, B, stride=Lc)
            nr = ar * xr - ai * xi + sr_ref[tiles, rows, :]
            ni = ar * xi + ai * xr + si_ref[tiles, rows, :]
            sr_ref[tiles, rows, :] = nr
            si_ref[tiles, rows, :] = ni
            return nr, ni

        xr, xi = lax.fori_loop(0, Lc, step, (xr_ref[tiles], xi_ref[tiles]), unroll=8)
        xr_ref[tiles] = xr
        xi_ref[tiles] = xi

    ys = []
    for j in range(SSM_BLOCKS):
        sr = jnp.concatenate([sr_ref[j * lpb + k] for k in range(lpb)], axis=-1).astype(jnp.bfloat16)
        si = jnp.concatenate([si_ref[j * lpb + k] for k in range(lpb)], axis=-1).astype(jnp.bfloat16)
        ys.append(jnp.dot(sr, cr_ref[j], preferred_element_type=jnp.float32)
                  + jnp.dot(si, ci_ref[j], preferred_element_type=jnp.float32))
    y = jnp.concatenate(ys, axis=-1) + d_ref[...] * u.astype(jnp.float32)
    y = _gelu_tanh(y).astype(jnp.bfloat16)
    vg = jnp.dot(y, wg_ref[...], preferred_element_type=jnp.float32)
    out = vg[:, :SSM_WIDTH] * _sigmoid(vg[:, SSM_WIDTH:])
    o_ref[...] = out.reshape(B, Lc, SSM_WIDTH).astype(o_ref.dtype)


def _ssm_tables(a_re, a_im, log_dt, b_re, b_im, c_re, c_im):
    f32 = jnp.float32
    lam = lax.complex(a_re.astype(f32), a_im.astype(f32))
    dt = jnp.exp(log_dt.astype(f32))[:, None]
    a_bar = jnp.exp(lam * dt)
    b_bar = ((a_bar - 1.0) / lam)[..., None] * lax.complex(b_re.astype(f32), b_im.astype(f32))
    gpb, P, N = SSM_GROUPS_PER_BLOCK, SSM_GROUP, SSM_STATE
    eye = jnp.eye(gpb, dtype=f32)

    def in_blocks(w):
        w = w.reshape(SSM_BLOCKS, gpb, N, P)
        return jnp.einsum('jgnp,gh->jgphn', w, eye).reshape(SSM_BLOCKS, gpb * P, gpb * N)

    def out_blocks(w):
        w = w.reshape(SSM_BLOCKS, gpb, P, N)
        return jnp.einsum('jgpn,gh->jgnhp', w, eye).reshape(SSM_BLOCKS, gpb * N, gpb * P)

    wbr = in_blocks(jnp.real(b_bar)).astype(jnp.bfloat16)
    wbi = in_blocks(jnp.imag(b_bar)).astype(jnp.bfloat16)
    cr = out_blocks(c_re.astype(f32)).astype(jnp.bfloat16)
    ci = out_blocks(-c_im.astype(f32)).astype(jnp.bfloat16)
    ar = jnp.real(a_bar).reshape(SSM_STATE_TILES, 1, LANES)
    ai = jnp.imag(a_bar).reshape(SSM_STATE_TILES, 1, LANES)
    return wbr, wbi, ar, ai, cr, ci


def _ssm(proj, tables, d_skip, w_glu, B, S):
    wbr, wbi, ar, ai, cr, ci = tables
    n_in = proj.shape[1]
    Lc = _tile(S, 256, BF16_SUBLANES)
    full = lambda *shape: pl.BlockSpec(shape, lambda c: (0,) * len(shape))
    return pl.pallas_call(
        functools.partial(_ssm_kernel, B=B, Lc=Lc),
        grid=(S // Lc,),
        in_specs=[pl.BlockSpec((B, Lc, SSM_WIDTH), lambda c: (0, c, OFF_US // SSM_WIDTH)),
                  full(*wbr.shape), full(*wbi.shape), full(*ar.shape), full(*ai.shape),
                  full(*cr.shape), full(*ci.shape), full(1, SSM_WIDTH), full(*w_glu.shape)],
        out_specs=pl.BlockSpec((B, Lc, SSM_WIDTH), lambda c: (0, c, 0)),
        out_shape=jax.ShapeDtypeStruct((B, S, SSM_WIDTH), jnp.bfloat16),
        scratch_shapes=[pltpu.VMEM((SSM_STATE_TILES, B * Lc, LANES), jnp.float32),
                        pltpu.VMEM((SSM_STATE_TILES, B * Lc, LANES), jnp.float32),
                        pltpu.VMEM((SSM_STATE_TILES, B, LANES), jnp.float32),
                        pltpu.VMEM((SSM_STATE_TILES, B, LANES), jnp.float32)],
        compiler_params=_params("arbitrary"),
        name="s5_ssm",
    )(proj.reshape(B, S, n_in), wbr, wbi, ar, ai, cr, ci, d_skip.reshape(1, SSM_WIDTH), w_glu)


def _sb_kernel(q_ref, k_ref, v_ref, o_ref, *, tq):
    qi = pl.program_id(2)
    q = q_ref[...]
    row = lax.broadcasted_iota(jnp.int32, (tq, tq), 0)
    col = lax.broadcasted_iota(jnp.int32, (tq, tq), 1)
    after = jnp.where(row > col, 1.0, 0.0).astype(jnp.bfloat16)
    causal = col < row
    dn = (((1,), (1,)), ((), ()))
    scale = HEAD_DIM_SB ** -0.5

    def block(j, carry, acc, diagonal):
        start = pl.multiple_of(j * tq, tq)
        k = k_ref[pl.ds(start, tq), :]
        v = v_ref[pl.ds(start, tq), :]
        z = lax.dot_general(q, k, dn, preferred_element_type=jnp.float32) * scale
        log_keep = -(jnp.maximum(z, 0.0) + jnp.log(1.0 + jnp.exp(-jnp.abs(z))))
        log_beta = z + log_keep
        if diagonal:
            log_keep = jnp.where(causal, log_keep, 0.0)
        hi = log_keep.astype(jnp.bfloat16)
        lo = (log_keep - hi.astype(jnp.float32)).astype(jnp.bfloat16)
        between = (jnp.dot(hi, after, preferred_element_type=jnp.float32)
                   + jnp.dot(lo, after, preferred_element_type=jnp.float32))
        w = jnp.exp(log_beta + between + carry)
        if diagonal:
            w = jnp.where(causal, w, 0.0)
        acc = acc + jnp.dot(w.astype(jnp.bfloat16), v, preferred_element_type=jnp.float32)
        carry = carry + jnp.sum(log_keep, axis=-1, keepdims=True)
        return carry, acc

    carry, acc = block(qi, jnp.zeros((tq, 1), jnp.float32),
                       jnp.zeros((tq, HEAD_DIM_SB), jnp.float32), True)

    def body(it, state):
        return block(qi - 1 - it, state[0], state[1], False)

    carry, acc = lax.fori_loop(0, qi, body, (carry, acc))
    o_ref[...] = acc.astype(o_ref.dtype)


def _stick_breaking(proj, B, S):
    tq = _tile(S, 256, LANES)
    nq = S // tq
    hd = HEAD_DIM_SB
    return pl.pallas_call(
        functools.partial(_sb_kernel, tq=tq),
        grid=(B, N_SB, nq),
        in_specs=[pl.BlockSpec((tq, hd), lambda b, h, i: (b * nq + i, OFF_QS // hd + h)),
                  pl.BlockSpec((S, hd), lambda b, h, i: (b, OFF_KS // hd + h)),
                  pl.BlockSpec((S, hd), lambda b, h, i: (b, OFF_VS // hd + h))],
        out_specs=pl.BlockSpec((tq, hd), lambda b, h, i: (b * nq + i, h)),
        out_shape=jax.ShapeDtypeStruct((B * S, SB_W), jnp.bfloat16),
        compiler_params=_params("parallel", "parallel", "arbitrary"),
        name="stick_breaking",
    )(proj, proj, proj)


def _merge_kernel(ya_ref, ys_ref, yc_ref, w_ref, ga_ref, gs_ref, gc_ref, o_ref):
    acc = None
    for k, (y_ref, g_ref) in enumerate(((ya_ref, ga_ref), (ys_ref, gs_ref), (yc_ref, gc_ref))):
        t = jnp.dot(y_ref[...], w_ref[k], preferred_element_type=jnp.float32)
        t = _sigmoid(g_ref[...].astype(jnp.float32)) * t
        acc = t if acc is None else acc + t
    o_ref[...] = acc.astype(o_ref.dtype)


def _merge(y_a, y_s, y_c, w_branch, proj, D):
    M = y_a.shape[0]
    tm = _tile(M, 1024, BF16_SUBLANES)
    tn = _tile(math.gcd(D, OFF_G), 512, LANES)
    gate = lambda k: (lambda m, n: (m, (OFF_G + k * D) // tn + n))
    row = pl.BlockSpec((tm, BRANCH_WIDTH), lambda m, n: (m, 0))
    return pl.pallas_call(
        _merge_kernel,
        grid=(M // tm, D // tn),
        in_specs=[row, row, row,
                  pl.BlockSpec((N_BRANCH, BRANCH_WIDTH, tn), lambda m, n: (0, 0, n)),
                  pl.BlockSpec((tm, tn), gate(0)),
                  pl.BlockSpec((tm, tn), gate(1)),
                  pl.BlockSpec((tm, tn), gate(2))],
        out_specs=pl.BlockSpec((tm, tn), lambda m, n: (m, n)),
        out_shape=jax.ShapeDtypeStruct((M, D), jnp.bfloat16),
        compiler_params=_params("parallel", "arbitrary"),
        name="gated_merge",
    )(y_a, y_s, y_c, w_branch, proj, proj, proj)


def _ffn_up_kernel(h_ref, halo_ref, wa_ref, wb_ref, cwa_ref, cwb_ref, cba_ref, cbb_ref,
                   o_ref, a_scr, *, tm, tiles_per_seq):
    H = BF16_SUBLANES

    @pl.when(pl.program_id(1) == 0)
    def _():
        first = pl.program_id(0) % tiles_per_seq == 0
        halo = halo_ref[...]
        a_scr[0:H, :] = jnp.where(first, jnp.zeros_like(halo), halo)
        a_scr[H:, :] = h_ref[...]

    a = a_scr[...]

    def conv(w_ref, cw_ref, cb_ref):
        u = jnp.dot(a, w_ref[...], preferred_element_type=jnp.float32)
        u1 = pltpu.roll(u, 1, 0)
        u2 = pltpu.roll(u, 2, 0)
        return (cb_ref[...] + cw_ref[0:1, :] * u2[H:] + cw_ref[1:2, :] * u1[H:]
                + cw_ref[2:3, :] * u[H:])

    ga = conv(wa_ref, cwa_ref, cba_ref)
    gb = conv(wb_ref, cwb_ref, cbb_ref)
    o_ref[...] = (_gelu_tanh(ga) * gb).astype(o_ref.dtype)


def _ffn_up(h, w_up, conv_w, conv_b, S):
    M, D = h.shape
    F = w_up.shape[1] // 2
    H = BF16_SUBLANES
    tm = _tile(S, 1024, H)
    tn = _tile(F, 512, LANES)
    nf = F // tn
    return pl.pallas_call(
        functools.partial(_ffn_up_kernel, tm=tm, tiles_per_seq=S // tm),
        grid=(M // tm, nf),
        in_specs=[pl.BlockSpec((tm, D), lambda m, n: (m, 0)),
                  pl.BlockSpec((H, D), lambda m, n: (jnp.maximum(m * (tm // H) - 1, 0), 0)),
                  pl.BlockSpec((D, tn), lambda m, n: (0, n)),
                  pl.BlockSpec((D, tn), lambda m, n: (0, nf + n)),
                  pl.BlockSpec((CONV_W, tn), lambda m, n: (0, n)),
                  pl.BlockSpec((CONV_W, tn), lambda m, n: (0, nf + n)),
                  pl.BlockSpec((1, tn), lambda m, n: (0, n)),
                  pl.BlockSpec((1, tn), lambda m, n: (0, nf + n))],
        out_specs=pl.BlockSpec((tm, tn), lambda m, n: (m, n)),
        out_shape=jax.ShapeDtypeStruct((M, F), jnp.bfloat16),
        scratch_shapes=[pltpu.VMEM((H + tm, D), jnp.bfloat16)],
        compiler_params=_params("parallel", "arbitrary"),
        name="ffn_up_conv_gate",
    )(h, h, w_up, w_up, conv_w, conv_w, conv_b.reshape(1, 2 * F), conv_b.reshape(1, 2 * F))


def _reorder_w_in(w, D):
    o = 0
    parts = {}
    for name, width in (("qa", Q_A), ("ka", KV_A), ("va", KV_A), ("us", SSM_WIDTH), ("qs", SB_W),
                        ("ks", SB_W), ("vs", SB_W), ("g", N_BRANCH * D)):
        parts[name] = w[:, o:o + width]
        o += width
    return jnp.concatenate([parts[n] for n in ("qa", "us", "qs", "ks", "vs", "g", "ka", "va")],
                           axis=1).astype(jnp.bfloat16)


def kernel(x, c, w_ada, b_ada, ada_table, norm_mix_pre, norm_mix_post, norm_ffn_pre, norm_ffn_post,
           w_in, attn_sinks, ssm_a_re, ssm_a_im, ssm_log_dt, ssm_b_re, ssm_b_im, ssm_c_re, ssm_c_im,
           ssm_d, ssm_w_glu, w_branch, w_out, ffn_w_up, ffn_conv_w, ffn_conv_b, ffn_w_down):
    B, S, D = x.shape
    depth = w_in.shape[0]
    T = B * S
    bf16 = jnp.bfloat16
    off_ka = OFF_G + N_BRANCH * D
    off_va = off_ka + KV_A

    mod_shared = _ada_proj(c, w_ada, b_ada)
    for l in range(depth):
        tab = ada_table[l]
        h = _normmod(x, norm_mix_pre[l], mod_shared, tab, 0)
        proj = _matmul(h.reshape(T, D), _reorder_w_in(w_in[l], D), bf16, 1024, 768, "in_proj")
        y_a = _swa(proj, attn_sinks[l], B, S, off_ka, off_va)
        tables = _ssm_tables(ssm_a_re[l], ssm_a_im[l], ssm_log_dt[l], ssm_b_re[l], ssm_b_im[l],
                             ssm_c_re[l], ssm_c_im[l])
        y_s = _ssm(proj, tables, ssm_d[l], ssm_w_glu[l].astype(bf16), B, S).reshape(T, SSM_WIDTH)
        y_c = _stick_breaking(proj, B, S)
        merged = _merge(y_a, y_s, y_c, w_branch[l].astype(bf16), proj, D)
        y = _matmul(merged, w_out[l].astype(bf16), jnp.float32, 1024, 1024, "out_proj")
        x = _postnorm_residual(x, y, norm_mix_post[l], mod_shared, tab, 2)

        h = _normmod(x, norm_ffn_pre[l], mod_shared, tab, 3)
        gact = _ffn_up(h.reshape(T, D), ffn_w_up[l].astype(bf16), ffn_conv_w[l], ffn_conv_b[l], S)
        y = _matmul(gact, ffn_w_down[l].astype(bf16), jnp.float32, 1024, 512, "ffn_down")
        x = _postnorm_residual(x, y, norm_ffn_post[l], mod_shared, tab, 5)
    return x
```

```python
import functools
import math

import jax
import jax.numpy as jnp
from jax import lax
from jax.experimental import pallas as pl
from jax.experimental.pallas import tpu as pltpu

HEAD_DIM_A = 64
N_Q_A = 16
N_KV_A = 2
WINDOW = 128
SSM_WIDTH = 1024
SSM_GROUP = 16
SSM_GROUPS = SSM_WIDTH // SSM_GROUP
SSM_STATE = 64
N_SB = 8
HEAD_DIM_SB = 128
BRANCH_WIDTH = 1024
N_BRANCH = 3
CONV_W = 3
N_MOD = 6
EPS = 1e-6
NEG_INF = -1e30

Q_A = N_Q_A * HEAD_DIM_A
KV_A = N_KV_A * HEAD_DIM_A
SB_W = N_SB * HEAD_DIM_SB

OFF_QA = 0
OFF_US = OFF_QA + Q_A
OFF_QS = OFF_US + SSM_WIDTH
OFF_KS = OFF_QS + SB_W
OFF_VS = OFF_KS + SB_W
OFF_G = OFF_VS + SB_W

LANES = 128
BF16_SUBLANES = 16
VMEM_LIMIT_BYTES = 56 * 1024 * 1024

SSM_GROUPS_PER_BLOCK = LANES // SSM_GROUP
SSM_BLOCKS = SSM_GROUPS // SSM_GROUPS_PER_BLOCK
SSM_STATES_PER_BLOCK = SSM_GROUPS_PER_BLOCK * SSM_STATE
SSM_STATES = SSM_GROUPS * SSM_STATE
SSM_SCAN_COLS = 1024
SSM_SCAN_TILES = SSM_SCAN_COLS // LANES
SSM_STATE_TILES = SSM_STATES // LANES


def _tile(dim, pref, align):
    t = (min(pref, dim) // align) * align
    while t >= align:
        if dim % t == 0:
            return t
        t -= align
    return dim


def _params(*sem):
    return pltpu.CompilerParams(dimension_semantics=sem, vmem_limit_bytes=VMEM_LIMIT_BYTES)


def _gelu_tanh(x):
    return 0.5 * x * (1.0 + jnp.tanh(math.sqrt(2.0 / math.pi) * (x + 0.044715 * (x * x * x))))


def _sigmoid(x):
    return 1.0 / (1.0 + jnp.exp(-x))


def _ada_kernel(c_ref, w_ref, b_ref, o_ref):
    c = c_ref[...]
    a = (c * _sigmoid(c)).astype(jnp.bfloat16)
    o_ref[...] = jnp.dot(a, w_ref[...].astype(jnp.bfloat16),
                         preferred_element_type=jnp.float32) + b_ref[...]


def _ada_proj(c, w_ada, b_ada):
    B, D = c.shape
    N = w_ada.shape[1]
    rows = 8
    c_pad = jnp.zeros((rows, D), jnp.float32).at[:B].set(c)
    tn = _tile(N, 512, LANES)
    out = pl.pallas_call(
        _ada_kernel,
        grid=(N // tn,),
        in_specs=[pl.BlockSpec((rows, D), lambda n: (0, 0)),
                  pl.BlockSpec((D, tn), lambda n: (0, n)),
                  pl.BlockSpec((1, tn), lambda n: (0, n))],
        out_specs=pl.BlockSpec((rows, tn), lambda n: (0, n)),
        out_shape=jax.ShapeDtypeStruct((rows, N), jnp.float32),
        compiler_params=_params("parallel"),
        name="ada_proj",
    )(c_pad, w_ada, b_ada.reshape(1, N))
    return out[:B].reshape(B, N_MOD, D)


def _normmod_kernel(x_ref, g_ref, mod_ref, tab_ref, h_ref, *, shift_idx):
    x = x_ref[0]
    ms = jnp.mean(x * x, axis=-1, keepdims=True)
    y = x * lax.rsqrt(ms + EPS) * g_ref[...]
    shift = mod_ref[0, shift_idx:shift_idx + 1, :] + tab_ref[shift_idx:shift_idx + 1, :]
    scale = mod_ref[0, shift_idx + 1:shift_idx + 2, :] + tab_ref[shift_idx + 1:shift_idx + 2, :]
    h_ref[0] = (y * (1.0 + scale) + shift).astype(h_ref.dtype)


def _normmod(x, g, mod_shared, tab, shift_idx):
    B, S, D = x.shape
    tr = _tile(S, 256, BF16_SUBLANES)
    return pl.pallas_call(
        functools.partial(_normmod_kernel, shift_idx=shift_idx),
        grid=(B, S // tr),
        in_specs=[pl.BlockSpec((1, tr, D), lambda b, i: (b, i, 0)),
                  pl.BlockSpec((1, D), lambda b, i: (0, 0)),
                  pl.BlockSpec((1, N_MOD, D), lambda b, i: (b, 0, 0)),
                  pl.BlockSpec((N_MOD, D), lambda b, i: (0, 0))],
        out_specs=pl.BlockSpec((1, tr, D), lambda b, i: (b, i, 0)),
        out_shape=jax.ShapeDtypeStruct((B, S, D), jnp.bfloat16),
        compiler_params=_params("parallel", "parallel"),
        name="normmod",
    )(x, g.reshape(1, D), mod_shared, tab)


def _postnorm_kernel(x_ref, y_ref, g_ref, mod_ref, tab_ref, *rest, gate_idx, next_shift_idx):
    y = y_ref[0].astype(jnp.float32)
    ms = jnp.mean(y * y, axis=-1, keepdims=True)
    yn = y * lax.rsqrt(ms + EPS) * g_ref[...]
    gate = mod_ref[0, gate_idx:gate_idx + 1, :] + tab_ref[gate_idx:gate_idx + 1, :]
    x = x_ref[0] + gate * yn
    if next_shift_idx is None:
        (o_ref,) = rest
        o_ref[0] = x
        return
    gn_ref, tabn_ref, o_ref, h_ref = rest
    o_ref[0] = x
    ms = jnp.mean(x * x, axis=-1, keepdims=True)
    xn = x * lax.rsqrt(ms + EPS) * gn_ref[...]
    k = next_shift_idx
    shift = mod_ref[0, k:k + 1, :] + tabn_ref[k:k + 1, :]
    scale = mod_ref[0, k + 1:k + 2, :] + tabn_ref[k + 1:k + 2, :]
    h_ref[0] = (xn * (1.0 + scale) + shift).astype(h_ref.dtype)


def _postnorm_residual(x, y, g, mod_shared, tab, gate_idx, next_norm=None):
    B, S, D = x.shape
    tr = _tile(S, 256, BF16_SUBLANES)
    row = pl.BlockSpec((1, tr, D), lambda b, i: (b, i, 0))
    vec = pl.BlockSpec((1, D), lambda b, i: (0, 0))
    tabs = pl.BlockSpec((N_MOD, D), lambda b, i: (0, 0))
    in_specs = [row, row, vec, pl.BlockSpec((1, N_MOD, D), lambda b, i: (b, 0, 0)), tabs]
    args = [x, y.reshape(B, S, D), g.reshape(1, D), mod_shared, tab]
    x_shape = jax.ShapeDtypeStruct((B, S, D), jnp.float32)
    if next_norm is None:
        out_specs, out_shape, next_shift_idx = row, x_shape, None
    else:
        g_next, tab_next, next_shift_idx = next_norm
        in_specs += [vec, tabs]
        args += [g_next.reshape(1, D), tab_next]
        out_specs = (row, row)
        out_shape = (x_shape, jax.ShapeDtypeStruct((B, S, D), jnp.bfloat16))
    return pl.pallas_call(
        functools.partial(_postnorm_kernel, gate_idx=gate_idx, next_shift_idx=next_shift_idx),
        grid=(B, S // tr),
        in_specs=in_specs,
        out_specs=out_specs,
        out_shape=out_shape,
        compiler_params=_params("parallel", "parallel"),
        name="postnorm_residual",
    )(*args)


def _mm_kernel(a_ref, w_ref, o_ref):
    o_ref[...] = jnp.dot(a_ref[...], w_ref[...],
                         preferred_element_type=jnp.float32).astype(o_ref.dtype)


def _matmul(a, w, out_dtype, tm_pref, tn_pref, name):
    M, K = a.shape
    N = w.shape[1]
    tm = _tile(M, tm_pref, BF16_SUBLANES)
    tn = _tile(N, tn_pref, LANES)
    return pl.pallas_call(
        _mm_kernel,
        grid=(M // tm, N // tn),
        in_specs=[pl.BlockSpec((tm, K), lambda m, n: (m, 0)),
                  pl.BlockSpec((K, tn), lambda m, n: (0, n))],
        out_specs=pl.BlockSpec((tm, tn), lambda m, n: (m, n)),
        out_shape=jax.ShapeDtypeStruct((M, N), out_dtype),
        compiler_params=_params("parallel", "arbitrary"),
        name=name,
    )(a, w)


def _swa_kernel(sink_ref, biasp_ref, biasc_ref, q_ref, kp_ref, kc_ref, vp_ref, vc_ref, o_ref):
    i = pl.program_id(1)
    W = WINDOW
    Dh = HEAD_DIM_A
    G = N_Q_A // N_KV_A
    n_pairs = N_Q_A // 2
    dn = (((1,), (1,)), ((), ()))
    zeros = jnp.zeros((W, Dh), jnp.bfloat16)

    def blockdiag(x):
        return jnp.concatenate([jnp.concatenate([x, zeros], axis=1),
                                jnp.concatenate([zeros, x], axis=1)], axis=0)

    kbd_p, kbd_c, vbd_p, vbd_c = [], [], [], []
    for hk in range(N_KV_A):
        sl = slice(hk * Dh, (hk + 1) * Dh)
        kbd_p.append(blockdiag(kp_ref[:, sl]))
        kbd_c.append(blockdiag(kc_ref[:, sl]))
        vbd_p.append(blockdiag(vp_ref[:, sl]))
        vbd_c.append(blockdiag(vc_ref[:, sl]))

    no_prev = jnp.where(i > 0, 0.0, NEG_INF)
    scores = []
    for p in range(n_pairs):
        hk = (2 * p) // G
        q = q_ref[:, p * 2 * Dh:(p + 1) * 2 * Dh]
        scores.append((lax.dot_general(q, kbd_p[hk], dn, preferred_element_type=jnp.float32),
                       lax.dot_general(q, kbd_c[hk], dn, preferred_element_type=jnp.float32)))

    lane = lax.broadcasted_iota(jnp.int32, (W, 2 * Dh), 1)
    for p in range(n_pairs):
        hk = (2 * p) // G
        tp = scores[p][0] * (Dh ** -0.5) + (biasp_ref[p] + no_prev)
        tc = scores[p][1] * (Dh ** -0.5) + biasc_ref[p]
        eps, ecs, invs = [], [], []
        for half in range(2):
            sink = sink_ref[2 * p + half]
            hl = slice(half * W, (half + 1) * W)
            m = jnp.maximum(jnp.max(jnp.maximum(tp[:, hl], tc[:, hl]), axis=-1, keepdims=True), sink)
            ep = jnp.exp(tp[:, hl] - m)
            ec = jnp.exp(tc[:, hl] - m)
            denom = jnp.sum(ep + ec, axis=-1, keepdims=True) + jnp.exp(sink - m)
            eps.append(ep.astype(jnp.bfloat16))
            ecs.append(ec.astype(jnp.bfloat16))
            invs.append(1.0 / denom)
        o = (jnp.dot(jnp.concatenate(eps, axis=1), vbd_p[hk], preferred_element_type=jnp.float32)
             + jnp.dot(jnp.concatenate(ecs, axis=1), vbd_c[hk], preferred_element_type=jnp.float32))
        o = o * jnp.where(lane < Dh, invs[0], invs[1])
        o_ref[:, p * 2 * Dh:(p + 1) * 2 * Dh] = o.astype(o_ref.dtype)


def _swa_bias():
    W = WINDOW
    r = jnp.arange(W)[:, None]
    c = jnp.arange(W)[None, :]
    slopes = jnp.exp2(-8.0 * jnp.arange(1, N_Q_A + 1, dtype=jnp.float32) / N_Q_A)[:, None, None]
    bp = jnp.where(c > r, -slopes * (r + W - c).astype(jnp.float32), NEG_INF)
    bc = jnp.where(c <= r, -slopes * (r - c).astype(jnp.float32), NEG_INF)
    pair = lambda b: b.reshape(N_Q_A // 2, 2, W, W).transpose(0, 2, 1, 3).reshape(N_Q_A // 2, W, 2 * W)
    return pair(bp), pair(bc)


def _swa(proj, sinks, B, S, off_ka, off_va):
    nb = S // WINDOW
    kb = off_ka // KV_A
    vb = off_va // KV_A
    cur = lambda col: (lambda b, i: (b * nb + i, col))
    prev = lambda col: (lambda b, i: (b * nb + jnp.maximum(i - 1, 0), col))
    biasp, biasc = _swa_bias()
    bias_spec = pl.BlockSpec(biasp.shape, lambda b, i: (0, 0, 0))
    return pl.pallas_call(
        _swa_kernel,
        grid=(B, nb),
        in_specs=[pl.BlockSpec(memory_space=pltpu.SMEM), bias_spec, bias_spec,
                  pl.BlockSpec((WINDOW, Q_A), cur(OFF_QA // Q_A)),
                  pl.BlockSpec((WINDOW, KV_A), prev(kb)),
                  pl.BlockSpec((WINDOW, KV_A), cur(kb)),
                  pl.BlockSpec((WINDOW, KV_A), prev(vb)),
                  pl.BlockSpec((WINDOW, KV_A), cur(vb))],
        out_specs=pl.BlockSpec((WINDOW, Q_A), lambda b, i: (b * nb + i, 0)),
        out_shape=jax.ShapeDtypeStruct((B * S, Q_A), jnp.bfloat16),
        compiler_params=_params("parallel", "parallel"),
        name="swa",
    )(sinks, biasp, biasc, proj, proj, proj, proj, proj)


def _ssm_kernel(u_ref, wbr_ref, wbi_ref, ar_ref, ai_ref, cr_ref, ci_ref, d_ref, wg_ref,
                o_ref, sr_ref, si_ref, xr_ref, xi_ref, *, B, Lc):
    @pl.when(pl.program_id(0) == 0)
    def _():
        xr_ref[...] = jnp.zeros_like(xr_ref)
        xi_ref[...] = jnp.zeros_like(xi_ref)

    u = u_ref[...].reshape(B * Lc, SSM_WIDTH)
    lpb = SSM_STATES_PER_BLOCK // LANES
    for j in range(SSM_BLOCKS):
        uj = u[:, j * LANES:(j + 1) * LANES]
        br = jnp.dot(uj, wbr_ref[j], preferred_element_type=jnp.float32)
        bi = jnp.dot(uj, wbi_ref[j], preferred_element_type=jnp.float32)
        for k in range(lpb):
            sr_ref[j * lpb + k] = br[:, k * LANES:(k + 1) * LANES]
            si_ref[j * lpb + k] = bi[:, k * LANES:(k + 1) * LANES]

    for cc in range(SSM_STATES // SSM_SCAN_COLS):
        tiles = pl.ds(cc * SSM_SCAN_TILES, SSM_SCAN_TILES)
        ar = jnp.broadcast_to(ar_ref[tiles], (SSM_SCAN_TILES, B, LANES))
        ai = jnp.broadcast_to(ai_ref[tiles], (SSM_SCAN_TILES, B, LANES))

        def step(t, carry):
            xr, xi = carry
            rows = pl.ds(t, B, stride=Lc)
            nr = ar * xr - ai * xi + sr_ref[tiles, rows, :]
            ni = ar * xi + ai * xr + si_ref[tiles, rows, :]
            sr_ref[tiles, rows, :] = nr
            si_ref[tiles, rows, :] = ni
            return nr, ni

        xr, xi = lax.fori_loop(0, Lc, step, (xr_ref[tiles], xi_ref[tiles]), unroll=8)
        xr_ref[tiles] = xr
        xi_ref[tiles] = xi

    ys = []
    for j in range(SSM_BLOCKS):
        sr = jnp.concatenate([sr_ref[j * lpb + k] for k in range(lpb)], axis=-1).astype(jnp.bfloat16)
        si = jnp.concatenate([si_ref[j * lpb + k] for k in range(lpb)], axis=-1).astype(jnp.bfloat16)
        ys.append(jnp.dot(sr, cr_ref[j], preferred_element_type=jnp.float32)
                  + jnp.dot(si, ci_ref[j], preferred_element_type=jnp.float32))
    y = jnp.concatenate(ys, axis=-1) + d_ref[...] * u.astype(jnp.float32)
    y = _gelu_tanh(y).astype(jnp.bfloat16)
    vg = jnp.dot(y, wg_ref[...], preferred_element_type=jnp.float32)
    out = vg[:, :SSM_WIDTH] * _sigmoid(vg[:, SSM_WIDTH:])
    o_ref[...] = out.reshape(B, Lc, SSM_WIDTH).astype(o_ref.dtype)


def _ssm_tables(a_re, a_im, log_dt, b_re, b_im, c_re, c_im):
    f32 = jnp.float32
    lr, li = a_re.astype(f32), a_im.astype(f32)
    dt = jnp.exp(log_dt.astype(f32))[:, None]
    mag = jnp.exp(lr * dt)
    abr, abi = mag * jnp.cos(li * dt), mag * jnp.sin(li * dt)
    den = lr * lr + li * li
    kr = ((abr - 1.0) * lr + abi * li) / den
    ki = (abi * lr - (abr - 1.0) * li) / den
    br, bi = b_re.astype(f32), b_im.astype(f32)
    bbr = kr[..., None] * br - ki[..., None] * bi
    bbi = kr[..., None] * bi + ki[..., None] * br
    gpb, P, N = SSM_GROUPS_PER_BLOCK, SSM_GROUP, SSM_STATE
    eye = jnp.eye(gpb, dtype=f32)

    def in_blocks(w):
        w = w.reshape(SSM_BLOCKS, gpb, N, P)
        return jnp.einsum('jgnp,gh->jgphn', w, eye).reshape(SSM_BLOCKS, gpb * P, gpb * N)

    def out_blocks(w):
        w = w.reshape(SSM_BLOCKS, gpb, P, N)
        return jnp.einsum('jgpn,gh->jgnhp', w, eye).reshape(SSM_BLOCKS, gpb * N, gpb * P)

    wbr = in_blocks(bbr).astype(jnp.bfloat16)
    wbi = in_blocks(bbi).astype(jnp.bfloat16)
    cr = out_blocks(c_re.astype(f32)).astype(jnp.bfloat16)
    ci = out_blocks(-c_im.astype(f32)).astype(jnp.bfloat16)
    ar = abr.reshape(SSM_STATE_TILES, 1, LANES)
    ai = abi.reshape(SSM_STATE_TILES, 1, LANES)
    return wbr, wbi, ar, ai, cr, ci


def _ssm(proj, tables, d_skip, w_glu, B, S):
    wbr, wbi, ar, ai, cr, ci = tables
    n_in = proj.shape[1]
    Lc = _tile(S, 256, BF16_SUBLANES)
    full = lambda *shape: pl.BlockSpec(shape, lambda c: (0,) * len(shape))
    return pl.pallas_call(
        functools.partial(_ssm_kernel, B=B, Lc=Lc),
        grid=(S // Lc,),
        in_specs=[pl.BlockSpec((B, Lc, SSM_WIDTH), lambda c: (0, c, OFF_US // SSM_WIDTH)),
                  full(*wbr.shape), full(*wbi.shape), full(*ar.shape), full(*ai.shape),
                  full(*cr.shape), full(*ci.shape), full(1, SSM_WIDTH), full(*w_glu.shape)],
        out_specs=pl.BlockSpec((B, Lc, SSM_WIDTH), lambda c: (0, c, 0)),
        out_shape=jax.ShapeDtypeStruct((B, S, SSM_WIDTH), jnp.bfloat16),
        scratch_shapes=[pltpu.VMEM((SSM_STATE_TILES, B * Lc, LANES), jnp.float32),
                        pltpu.VMEM((SSM_STATE_TILES, B * Lc, LANES), jnp.float32),
                        pltpu.VMEM((SSM_STATE_TILES, B, LANES), jnp.float32),
                        pltpu.VMEM((SSM_STATE_TILES, B, LANES), jnp.float32)],
        compiler_params=_params("arbitrary"),
        name="s5_ssm",
    )(proj.reshape(B, S, n_in), wbr, wbi, ar, ai, cr, ci, d_skip.reshape(1, SSM_WIDTH), w_glu)


def _sb_kernel(q_ref, k_ref, v_ref, o_ref, *, tq, heads):
    qi = pl.program_id(2)
    hd = HEAD_DIM_SB
    row = lax.broadcasted_iota(jnp.int32, (tq, tq), 0)
    col = lax.broadcasted_iota(jnp.int32, (tq, tq), 1)
    after = jnp.where(row > col, 1.0, 0.0).astype(jnp.bfloat16)
    after2 = jnp.concatenate([after, after], axis=0)
    causal = col < row
    dn = (((1,), (1,)), ((), ()))
    scale = hd ** -0.5
    log2e = math.log2(math.e)
    ln2 = math.log(2.0)

    def blocks(j, state, diagonal):
        start = pl.multiple_of(j * tq, tq)
        zs = []
        for h in range(heads):
            k = k_ref[pl.ds(start, tq), h * hd:(h + 1) * hd]
            zs.append(lax.dot_general(qs[h], k, dn, preferred_element_type=jnp.float32) * scale)
        betweens, log_betas, sums = [], [], []
        for h in range(heads):
            z = zs[h]
            sp = jnp.maximum(z, 0.0) + jnp.log(1.0 + jnp.exp2(jnp.abs(z) * (-log2e)))
            log_betas.append(z - sp)
            if diagonal:
                sp = jnp.where(causal, sp, 0.0)
            hi = sp.astype(jnp.bfloat16)
            lo = (sp - hi.astype(jnp.float32)).astype(jnp.bfloat16)
            betweens.append(jnp.dot(jnp.concatenate([hi, lo], axis=1), after2,
                                    preferred_element_type=jnp.float32))
            sums.append(jnp.sum(sp, axis=-1, keepdims=True))
        out = []
        for h in range(heads):
            carry, acc = state[2 * h], state[2 * h + 1]
            w = jnp.exp2((log_betas[h] - betweens[h] - carry) * log2e)
            if diagonal:
                w = jnp.where(causal, w, 0.0)
            v = v_ref[pl.ds(start, tq), h * hd:(h + 1) * hd]
            acc = acc + jnp.dot(w.astype(jnp.bfloat16), v, preferred_element_type=jnp.float32)
            out.extend((carry + sums[h], acc))
        return tuple(out)

    qs = [q_ref[:, h * hd:(h + 1) * hd] for h in range(heads)]
    state = blocks(qi, (jnp.zeros((tq, 1), jnp.float32), jnp.zeros((tq, hd), jnp.float32)) * heads, True)

    def body(it, state):
        return blocks(qi - 1 - it, state, False)

    state = lax.fori_loop(0, qi, body, state)
    for h in range(heads):
        o_ref[:, h * hd:(h + 1) * hd] = state[2 * h + 1].astype(o_ref.dtype)


def _stick_breaking(proj, B, S):
    tq = _tile(S, 256, LANES)
    nq = S // tq
    heads = 4
    hd = heads * HEAD_DIM_SB
    return pl.pallas_call(
        functools.partial(_sb_kernel, tq=tq, heads=heads),
        grid=(B, N_SB // heads, nq),
        in_specs=[pl.BlockSpec((tq, hd), lambda b, h, i: (b * nq + i, OFF_QS // hd + h)),
                  pl.BlockSpec((S, hd), lambda b, h, i: (b, OFF_KS // hd + h)),
                  pl.BlockSpec((S, hd), lambda b, h, i: (b, OFF_VS // hd + h))],
        out_specs=pl.BlockSpec((tq, hd), lambda b, h, i: (b * nq + i, h)),
        out_shape=jax.ShapeDtypeStruct((B * S, SB_W), jnp.bfloat16),
        compiler_params=_params("parallel", "parallel", "arbitrary"),
        name="stick_breaking",
    )(proj, proj, proj)


def _merge_kernel(ya_ref, ys_ref, yc_ref, w_ref, ga_ref, gs_ref, gc_ref, o_ref):
    acc = None
    for k, (y_ref, g_ref) in enumerate(((ya_ref, ga_ref), (ys_ref, gs_ref), (yc_ref, gc_ref))):
        t = jnp.dot(y_ref[...], w_ref[k], preferred_element_type=jnp.float32)
        t = _sigmoid(g_ref[...].astype(jnp.float32)) * t
        acc = t if acc is None else acc + t
    o_ref[...] = acc.astype(o_ref.dtype)


def _merge(y_a, y_s, y_c, w_branch, proj, D):
    M = y_a.shape[0]
    tm = _tile(M, 1024, BF16_SUBLANES)
    tn = _tile(math.gcd(D, OFF_G), 512, LANES)
    gate = lambda k: (lambda m, n: (m, (OFF_G + k * D) // tn + n))
    row = pl.BlockSpec((tm, BRANCH_WIDTH), lambda m, n: (m, 0))
    return pl.pallas_call(
        _merge_kernel,
        grid=(M // tm, D // tn),
        in_specs=[row, row, row,
                  pl.BlockSpec((N_BRANCH, BRANCH_WIDTH, tn), lambda m, n: (0, 0, n)),
                  pl.BlockSpec((tm, tn), gate(0)),
                  pl.BlockSpec((tm, tn), gate(1)),
                  pl.BlockSpec((tm, tn), gate(2))],
        out_specs=pl.BlockSpec((tm, tn), lambda m, n: (m, n)),
        out_shape=jax.ShapeDtypeStruct((M, D), jnp.bfloat16),
        compiler_params=_params("parallel", "arbitrary"),
        name="gated_merge",
    )(y_a, y_s, y_c, w_branch, proj, proj, proj)


def _ffn_up_kernel(h_ref, halo_ref, wa_ref, wb_ref, cwa_ref, cwb_ref, cba_ref, cbb_ref,
                   o_ref, a_scr, *, tm, tiles_per_seq):
    H = BF16_SUBLANES

    @pl.when(pl.program_id(1) == 0)
    def _():
        first = pl.program_id(0) % tiles_per_seq == 0
        halo = halo_ref[...]
        a_scr[0:H, :] = jnp.where(first, jnp.zeros_like(halo), halo)
        a_scr[H:, :] = h_ref[...]

    a = a_scr[...]

    def conv(w_ref, cw_ref, cb_ref):
        u = jnp.dot(a, w_ref[...], preferred_element_type=jnp.float32)
        u1 = pltpu.roll(u, 1, 0)
        u2 = pltpu.roll(u, 2, 0)
        return (cb_ref[...] + cw_ref[0:1, :] * u2[H:] + cw_ref[1:2, :] * u1[H:]
                + cw_ref[2:3, :] * u[H:])

    ga = conv(wa_ref, cwa_ref, cba_ref)
    gb = conv(wb_ref, cwb_ref, cbb_ref)
    o_ref[...] = (_gelu_tanh(ga) * gb).astype(o_ref.dtype)


def _ffn_up(h, w_up, conv_w, conv_b, S):
    M, D = h.shape
    F = w_up.shape[1] // 2
    H = BF16_SUBLANES
    tm = _tile(S, 1024, H)
    tn = _tile(F, 512, LANES)
    nf = F // tn
    return pl.pallas_call(
        functools.partial(_ffn_up_kernel, tm=tm, tiles_per_seq=S // tm),
        grid=(M // tm, nf),
        in_specs=[pl.BlockSpec((tm, D), lambda m, n: (m, 0)),
                  pl.BlockSpec((H, D), lambda m, n: (jnp.maximum(m * (tm // H) - 1, 0), 0)),
                  pl.BlockSpec((D, tn), lambda m, n: (0, n)),
                  pl.BlockSpec((D, tn), lambda m, n: (0, nf + n)),
                  pl.BlockSpec((CONV_W, tn), lambda m, n: (0, n)),
                  pl.BlockSpec((CONV_W, tn), lambda m, n: (0, nf + n)),
                  pl.BlockSpec((1, tn), lambda m, n: (0, n)),
                  pl.BlockSpec((1, tn), lambda m, n: (0, nf + n))],
        out_specs=pl.BlockSpec((tm, tn), lambda m, n: (m, n)),
        out_shape=jax.ShapeDtypeStruct((M, F), jnp.bfloat16),
        scratch_shapes=[pltpu.VMEM((H + tm, D), jnp.bfloat16)],
        compiler_params=_params("parallel", "arbitrary"),
        name="ffn_up_conv_gate",
    )(h, h, w_up, w_up, conv_w, conv_w, conv_b.reshape(1, 2 * F), conv_b.reshape(1, 2 * F))


def _reorder_w_in(w, D):
    o = 0
    parts = {}
    for name, width in (("qa", Q_A), ("ka", KV_A), ("va", KV_A), ("us", SSM_WIDTH), ("qs", SB_W),
                        ("ks", SB_W), ("vs", SB_W), ("g", N_BRANCH * D)):
        parts[name] = w[:, o:o + width]
        o += width
    return jnp.concatenate([parts[n] for n in ("qa", "us", "qs", "ks", "vs", "g", "ka", "va")],
                           axis=1).astype(jnp.bfloat16)


def kernel(x, c, w_ada, b_ada, ada_table, norm_mix_pre, norm_mix_post, norm_ffn_pre, norm_ffn_post,
           w_in, attn_sinks, ssm_a_re, ssm_a_im, ssm_log_dt, ssm_b_re, ssm_b_im, ssm_c_re, ssm_c_im,
           ssm_d, ssm_w_glu, w_branch, w_out, ffn_w_up, ffn_conv_w, ffn_conv_b, ffn_w_down):
    B, S, D = x.shape
    depth = w_in.shape[0]
    T = B * S
    bf16 = jnp.bfloat16
    off_ka = OFF_G + N_BRANCH * D
    off_va = off_ka + KV_A

    mod_shared = _ada_proj(c, w_ada, b_ada)
    h = _normmod(x, norm_mix_pre[0], mod_shared, ada_table[0], 0)
    for l in range(depth):
        tab = ada_table[l]
        proj = _matmul(h.reshape(T, D), _reorder_w_in(w_in[l], D), bf16, 1024, 768, "in_proj")
        y_a = _swa(proj, attn_sinks[l], B, S, off_ka, off_va)
        tables = _ssm_tables(ssm_a_re[l], ssm_a_im[l], ssm_log_dt[l], ssm_b_re[l], ssm_b_im[l],
                             ssm_c_re[l], ssm_c_im[l])
        y_s = _ssm(proj, tables, ssm_d[l], ssm_w_glu[l].astype(bf16), B, S).reshape(T, SSM_WIDTH)
        y_c = _stick_breaking(proj, B, S)
        merged = _merge(y_a, y_s, y_c, w_branch[l].astype(bf16), proj, D)
        y = _matmul(merged, w_out[l].astype(bf16), bf16, 1024, 1024, "out_proj")
        x, h = _postnorm_residual(x, y, norm_mix_post[l], mod_shared, tab, 2,
                                  next_norm=(norm_ffn_pre[l], tab, 3))

        gact = _ffn_up(h.reshape(T, D), ffn_w_up[l].astype(bf16), ffn_conv_w[l], ffn_conv_b[l], S)
        y = _matmul(gact, ffn_w_down[l].astype(bf16), bf16, 1024, 512, "ffn_down")
        if l + 1 < depth:
            x, h = _postnorm_residual(x, y, norm_ffn_post[l], mod_shared, tab, 5,
                                      next_norm=(norm_mix_pre[l + 1], ada_table[l + 1], 0))
        else:
            x = _postnorm_residual(x, y, norm_ffn_post[l], mod_shared, tab, 5)
    return x
```

```python
import functools
import math

import jax
import jax.numpy as jnp
from jax import lax
from jax.experimental import pallas as pl
from jax.experimental.pallas import tpu as pltpu

HEAD_DIM_A = 64
N_Q_A = 16
N_KV_A = 2
WINDOW = 128
SSM_WIDTH = 1024
SSM_GROUP = 16
SSM_GROUPS = SSM_WIDTH // SSM_GROUP
SSM_STATE = 64
N_SB = 8
HEAD_DIM_SB = 128
BRANCH_WIDTH = 1024
N_BRANCH = 3
CONV_W = 3
N_MOD = 6
EPS = 1e-6
NEG_INF = -1e30

SB_EXP_UNDERFLOW = 150.0

Q_A = N_Q_A * HEAD_DIM_A
KV_A = N_KV_A * HEAD_DIM_A
SB_W = N_SB * HEAD_DIM_SB

OFF_QA = 0
OFF_US = OFF_QA + Q_A
OFF_QS = OFF_US + SSM_WIDTH
OFF_KS = OFF_QS + SB_W
OFF_VS = OFF_KS + SB_W
OFF_G = OFF_VS + SB_W

LANES = 128
BF16_SUBLANES = 16
VMEM_LIMIT_BYTES = 56 * 1024 * 1024

SSM_GROUPS_PER_BLOCK = LANES // SSM_GROUP
SSM_BLOCKS = SSM_GROUPS // SSM_GROUPS_PER_BLOCK
SSM_STATES_PER_BLOCK = SSM_GROUPS_PER_BLOCK * SSM_STATE
SSM_STATES = SSM_GROUPS * SSM_STATE
SSM_SCAN_COLS = 1024
SSM_SCAN_TILES = SSM_SCAN_COLS // LANES
SSM_STATE_TILES = SSM_STATES // LANES


def _tile(dim, pref, align):
    t = (min(pref, dim) // align) * align
    while t >= align:
        if dim % t == 0:
            return t
        t -= align
    return dim


def _params(*sem):
    return pltpu.CompilerParams(dimension_semantics=sem, vmem_limit_bytes=VMEM_LIMIT_BYTES)


def _gelu_tanh(x):
    return 0.5 * x * (1.0 + jnp.tanh(math.sqrt(2.0 / math.pi) * (x + 0.044715 * (x * x * x))))


def _sigmoid(x):
    return 1.0 / (1.0 + jnp.exp(-x))


def _ada_kernel(c_ref, w_ref, b_ref, o_ref):
    c = c_ref[...]
    a = (c * _sigmoid(c)).astype(jnp.bfloat16)
    o_ref[...] = jnp.dot(a, w_ref[...].astype(jnp.bfloat16),
                         preferred_element_type=jnp.float32) + b_ref[...]


def _ada_proj(c, w_ada, b_ada):
    B, D = c.shape
    N = w_ada.shape[1]
    rows = 8
    c_pad = jnp.zeros((rows, D), jnp.float32).at[:B].set(c)
    tn = _tile(N, 512, LANES)
    out = pl.pallas_call(
        _ada_kernel,
        grid=(N // tn,),
        in_specs=[pl.BlockSpec((rows, D), lambda n: (0, 0)),
                  pl.BlockSpec((D, tn), lambda n: (0, n)),
                  pl.BlockSpec((1, tn), lambda n: (0, n))],
        out_specs=pl.BlockSpec((rows, tn), lambda n: (0, n)),
        out_shape=jax.ShapeDtypeStruct((rows, N), jnp.float32),
        compiler_params=_params("parallel"),
        name="ada_proj",
    )(c_pad, w_ada, b_ada.reshape(1, N))
    return out[:B].reshape(B, N_MOD, D)


def _normmod_kernel(x_ref, g_ref, mod_ref, tab_ref, h_ref, *, shift_idx):
    x = x_ref[0]
    ms = jnp.mean(x * x, axis=-1, keepdims=True)
    y = x * lax.rsqrt(ms + EPS) * g_ref[...]
    shift = mod_ref[0, shift_idx:shift_idx + 1, :] + tab_ref[shift_idx:shift_idx + 1, :]
    scale = mod_ref[0, shift_idx + 1:shift_idx + 2, :] + tab_ref[shift_idx + 1:shift_idx + 2, :]
    h_ref[0] = (y * (1.0 + scale) + shift).astype(h_ref.dtype)


def _normmod(x, g, mod_shared, tab, shift_idx):
    B, S, D = x.shape
    tr = _tile(S, 256, BF16_SUBLANES)
    return pl.pallas_call(
        functools.partial(_normmod_kernel, shift_idx=shift_idx),
        grid=(B, S // tr),
        in_specs=[pl.BlockSpec((1, tr, D), lambda b, i: (b, i, 0)),
                  pl.BlockSpec((1, D), lambda b, i: (0, 0)),
                  pl.BlockSpec((1, N_MOD, D), lambda b, i: (b, 0, 0)),
                  pl.BlockSpec((N_MOD, D), lambda b, i: (0, 0))],
        out_specs=pl.BlockSpec((1, tr, D), lambda b, i: (b, i, 0)),
        out_shape=jax.ShapeDtypeStruct((B, S, D), jnp.bfloat16),
        compiler_params=_params("parallel", "parallel"),
        name="normmod",
    )(x, g.reshape(1, D), mod_shared, tab)


def _postnorm_kernel(x_ref, y_ref, g_ref, mod_ref, tab_ref, *rest, gate_idx, next_shift_idx):
    y = y_ref[0].astype(jnp.float32)
    ms = jnp.mean(y * y, axis=-1, keepdims=True)
    yn = y * lax.rsqrt(ms + EPS) * g_ref[...]
    gate = mod_ref[0, gate_idx:gate_idx + 1, :] + tab_ref[gate_idx:gate_idx + 1, :]
    x = x_ref[0] + gate * yn
    if next_shift_idx is None:
        (o_ref,) = rest
        o_ref[0] = x
        return
    gn_ref, tabn_ref, o_ref, h_ref = rest
    o_ref[0] = x
    ms = jnp.mean(x * x, axis=-1, keepdims=True)
    xn = x * lax.rsqrt(ms + EPS) * gn_ref[...]
    k = next_shift_idx
    shift = mod_ref[0, k:k + 1, :] + tabn_ref[k:k + 1, :]
    scale = mod_ref[0, k + 1:k + 2, :] + tabn_ref[k + 1:k + 2, :]
    h_ref[0] = (xn * (1.0 + scale) + shift).astype(h_ref.dtype)


def _postnorm_residual(x, y, g, mod_shared, tab, gate_idx, next_norm=None):
    B, S, D = x.shape
    tr = _tile(S, 256, BF16_SUBLANES)
    row = pl.BlockSpec((1, tr, D), lambda b, i: (b, i, 0))
    vec = pl.BlockSpec((1, D), lambda b, i: (0, 0))
    tabs = pl.BlockSpec((N_MOD, D), lambda b, i: (0, 0))
    in_specs = [row, row, vec, pl.BlockSpec((1, N_MOD, D), lambda b, i: (b, 0, 0)), tabs]
    args = [x, y.reshape(B, S, D), g.reshape(1, D), mod_shared, tab]
    x_shape = jax.ShapeDtypeStruct((B, S, D), jnp.float32)
    if next_norm is None:
        out_specs, out_shape, next_shift_idx = row, x_shape, None
    else:
        g_next, tab_next, next_shift_idx = next_norm
        in_specs += [vec, tabs]
        args += [g_next.reshape(1, D), tab_next]
        out_specs = (row, row)
        out_shape = (x_shape, jax.ShapeDtypeStruct((B, S, D), jnp.bfloat16))
    return pl.pallas_call(
        functools.partial(_postnorm_kernel, gate_idx=gate_idx, next_shift_idx=next_shift_idx),
        grid=(B, S // tr),
        in_specs=in_specs,
        out_specs=out_specs,
        out_shape=out_shape,
        compiler_params=_params("parallel", "parallel"),
        name="postnorm_residual",
    )(*args)


def _mm_kernel(a_ref, w_ref, o_ref):
    o_ref[...] = jnp.dot(a_ref[...], w_ref[...],
                         preferred_element_type=jnp.float32).astype(o_ref.dtype)


def _matmul(a, w, l, out_dtype, tm_pref, tn_pref, name):
    M, K = a.shape
    N = w.shape[2]
    tm = _tile(M, tm_pref, BF16_SUBLANES)
    tn = _tile(N, tn_pref, LANES)
    return pl.pallas_call(
        _mm_kernel,
        grid=(M // tm, N // tn),
        in_specs=[pl.BlockSpec((tm, K), lambda m, n: (m, 0)),
                  pl.BlockSpec((None, K, tn), lambda m, n: (l, 0, n))],
        out_specs=pl.BlockSpec((tm, tn), lambda m, n: (m, n)),
        out_shape=jax.ShapeDtypeStruct((M, N), out_dtype),
        compiler_params=_params("parallel", "arbitrary"),
        name=name,
    )(a, w)


def _swa_kernel(sink_ref, biasp_ref, biasc_ref, q_ref, kp_ref, kc_ref, vp_ref, vc_ref, o_ref):
    i = pl.program_id(1)
    W = WINDOW
    Dh = HEAD_DIM_A
    G = N_Q_A // N_KV_A
    n_pairs = N_Q_A // 2
    dn = (((1,), (1,)), ((), ()))
    zeros = jnp.zeros((W, Dh), jnp.bfloat16)

    def blockdiag(x):
        return jnp.concatenate([jnp.concatenate([x, zeros], axis=1),
                                jnp.concatenate([zeros, x], axis=1)], axis=0)

    kbd_p, kbd_c, vbd_p, vbd_c = [], [], [], []
    for hk in range(N_KV_A):
        sl = slice(hk * Dh, (hk + 1) * Dh)
        kbd_p.append(blockdiag(kp_ref[:, sl]))
        kbd_c.append(blockdiag(kc_ref[:, sl]))
        vbd_p.append(blockdiag(vp_ref[:, sl]))
        vbd_c.append(blockdiag(vc_ref[:, sl]))

    no_prev = jnp.where(i > 0, 0.0, NEG_INF)
    scores = []
    for p in range(n_pairs):
        hk = (2 * p) // G
        q = q_ref[:, p * 2 * Dh:(p + 1) * 2 * Dh]
        scores.append((lax.dot_general(q, kbd_p[hk], dn, preferred_element_type=jnp.float32),
                       lax.dot_general(q, kbd_c[hk], dn, preferred_element_type=jnp.float32)))

    lane = lax.broadcasted_iota(jnp.int32, (W, 2 * Dh), 1)
    for p in range(n_pairs):
        hk = (2 * p) // G
        tp = scores[p][0] * (Dh ** -0.5) + (biasp_ref[p] + no_prev)
        tc = scores[p][1] * (Dh ** -0.5) + biasc_ref[p]
        eps, ecs, invs = [], [], []
        for half in range(2):
            sink = sink_ref[2 * p + half]
            hl = slice(half * W, (half + 1) * W)
            m = jnp.maximum(jnp.max(jnp.maximum(tp[:, hl], tc[:, hl]), axis=-1, keepdims=True), sink)
            ep = jnp.exp(tp[:, hl] - m)
            ec = jnp.exp(tc[:, hl] - m)
            denom = jnp.sum(ep + ec, axis=-1, keepdims=True) + jnp.exp(sink - m)
            eps.append(ep.astype(jnp.bfloat16))
            ecs.append(ec.astype(jnp.bfloat16))
            invs.append(1.0 / denom)
        o = (jnp.dot(jnp.concatenate(eps, axis=1), vbd_p[hk], preferred_element_type=jnp.float32)
             + jnp.dot(jnp.concatenate(ecs, axis=1), vbd_c[hk], preferred_element_type=jnp.float32))
        o = o * jnp.where(lane < Dh, invs[0], invs[1])
        o_ref[:, p * 2 * Dh:(p + 1) * 2 * Dh] = o.astype(o_ref.dtype)


def _swa_bias():
    W = WINDOW
    r = jnp.arange(W)[:, None]
    c = jnp.arange(W)[None, :]
    slopes = jnp.exp2(-8.0 * jnp.arange(1, N_Q_A + 1, dtype=jnp.float32) / N_Q_A)[:, None, None]
    bp = jnp.where(c > r, -slopes * (r + W - c).astype(jnp.float32), NEG_INF)
    bc = jnp.where(c <= r, -slopes * (r - c).astype(jnp.float32), NEG_INF)
    pair = lambda b: b.reshape(N_Q_A // 2, 2, W, W).transpose(0, 2, 1, 3).reshape(N_Q_A // 2, W, 2 * W)
    return pair(bp), pair(bc)


def _swa(proj, sinks, B, S, off_ka, off_va):
    nb = S // WINDOW
    kb = off_ka // KV_A
    vb = off_va // KV_A
    cur = lambda col: (lambda b, i: (b * nb + i, col))
    prev = lambda col: (lambda b, i: (b * nb + jnp.maximum(i - 1, 0), col))
    biasp, biasc = _swa_bias()
    bias_spec = pl.BlockSpec(biasp.shape, lambda b, i: (0, 0, 0))
    return pl.pallas_call(
        _swa_kernel,
        grid=(B, nb),
        in_specs=[pl.BlockSpec(memory_space=pltpu.SMEM), bias_spec, bias_spec,
                  pl.BlockSpec((WINDOW, Q_A), cur(OFF_QA // Q_A)),
                  pl.BlockSpec((WINDOW, KV_A), prev(kb)),
                  pl.BlockSpec((WINDOW, KV_A), cur(kb)),
                  pl.BlockSpec((WINDOW, KV_A), prev(vb)),
                  pl.BlockSpec((WINDOW, KV_A), cur(vb))],
        out_specs=pl.BlockSpec((WINDOW, Q_A), lambda b, i: (b * nb + i, 0)),
        out_shape=jax.ShapeDtypeStruct((B * S, Q_A), jnp.bfloat16),
        compiler_params=_params("parallel", "parallel"),
        name="swa",
    )(sinks, biasp, biasc, proj, proj, proj, proj, proj)


def _ssm_kernel(u_ref, wbr_ref, wbi_ref, ar_ref, ai_ref, cr_ref, ci_ref, d_ref, wg_ref,
                o_ref, sr_ref, si_ref, xr_ref, xi_ref, *, B, Lc):
    @pl.when(pl.program_id(0) == 0)
    def _():
        xr_ref[...] = jnp.zeros_like(xr_ref)
        xi_ref[...] = jnp.zeros_like(xi_ref)

    u = u_ref[...].reshape(B * Lc, SSM_WIDTH)
    lpb = SSM_STATES_PER_BLOCK // LANES
    for j in range(SSM_BLOCKS):
        uj = u[:, j * LANES:(j + 1) * LANES]
        br = jnp.dot(uj, wbr_ref[j], preferred_element_type=jnp.float32)
        bi = jnp.dot(uj, wbi_ref[j], preferred_element_type=jnp.float32)
        for k in range(lpb):
            sr_ref[j * lpb + k] = br[:, k * LANES:(k + 1) * LANES]
            si_ref[j * lpb + k] = bi[:, k * LANES:(k + 1) * LANES]

    for cc in range(SSM_STATES // SSM_SCAN_COLS):
        tiles = pl.ds(cc * SSM_SCAN_TILES, SSM_SCAN_TILES)
        ar = jnp.broadcast_to(ar_ref[tiles], (SSM_SCAN_TILES, B, LANES))
        ai = jnp.broadcast_to(ai_ref[tiles], (SSM_SCAN_TILES, B, LANES))

        def step(t, carry):
            xr, xi = carry
            rows = pl.ds(t, B, stride=Lc)
            nr = ar * xr - ai * xi + sr_ref[tiles, rows, :]
            ni = ar * xi + ai * xr + si_ref[tiles, rows, :]
            sr_ref[tiles, rows, :] = nr
            si_ref[tiles, rows, :] = ni
            return nr, ni

        xr, xi = lax.fori_loop(0, Lc, step, (xr_ref[tiles], xi_ref[tiles]), unroll=8)
        xr_ref[tiles] = xr
        xi_ref[tiles] = xi

    ys = []
    for j in range(SSM_BLOCKS):
        sr = jnp.concatenate([sr_ref[j * lpb + k] for k in range(lpb)], axis=-1).astype(jnp.bfloat16)
        si = jnp.concatenate([si_ref[j * lpb + k] for k in range(lpb)], axis=-1).astype(jnp.bfloat16)
        ys.append(jnp.dot(sr, cr_ref[j], preferred_element_type=jnp.float32)
                  + jnp.dot(si, ci_ref[j], preferred_element_type=jnp.float32))
    y = jnp.concatenate(ys, axis=-1) + d_ref[...] * u.astype(jnp.float32)
    y = _gelu_tanh(y).astype(jnp.bfloat16)
    vg = jnp.dot(y, wg_ref[...], preferred_element_type=jnp.float32)
    out = vg[:, :SSM_WIDTH] * _sigmoid(vg[:, SSM_WIDTH:])
    o_ref[...] = out.reshape(B, Lc, SSM_WIDTH).astype(o_ref.dtype)


def _ssm_tables(a_re, a_im, log_dt, b_re, b_im, c_re, c_im):
    f32 = jnp.float32
    lr, li = a_re.astype(f32), a_im.astype(f32)
    dt = jnp.exp(log_dt.astype(f32))[:, None]
    mag = jnp.exp(lr * dt)
    abr, abi = mag * jnp.cos(li * dt), mag * jnp.sin(li * dt)
    den = lr * lr + li * li
    kr = ((abr - 1.0) * lr + abi * li) / den
    ki = (abi * lr - (abr - 1.0) * li) / den
    br, bi = b_re.astype(f32), b_im.astype(f32)
    bbr = kr[..., None] * br - ki[..., None] * bi
    bbi = kr[..., None] * bi + ki[..., None] * br
    gpb, P, N = SSM_GROUPS_PER_BLOCK, SSM_GROUP, SSM_STATE
    eye = jnp.eye(gpb, dtype=f32)

    def in_blocks(w):
        w = w.reshape(SSM_BLOCKS, gpb, N, P)
        return jnp.einsum('jgnp,gh->jgphn', w, eye).reshape(SSM_BLOCKS, gpb * P, gpb * N)

    def out_blocks(w):
        w = w.reshape(SSM_BLOCKS, gpb, P, N)
        return jnp.einsum('jgpn,gh->jgnhp', w, eye).reshape(SSM_BLOCKS, gpb * N, gpb * P)

    wbr = in_blocks(bbr).astype(jnp.bfloat16)
    wbi = in_blocks(bbi).astype(jnp.bfloat16)
    cr = out_blocks(c_re.astype(f32)).astype(jnp.bfloat16)
    ci = out_blocks(-c_im.astype(f32)).astype(jnp.bfloat16)
    ar = abr.reshape(SSM_STATE_TILES, 1, LANES)
    ai = abi.reshape(SSM_STATE_TILES, 1, LANES)
    return wbr, wbi, ar, ai, cr, ci


def _ssm(proj, tables, d_skip, w_glu, l, B, S):
    wbr, wbi, ar, ai, cr, ci = tables
    n_in = proj.shape[1]
    Lc = _tile(S, 256, BF16_SUBLANES)
    full = lambda *shape: pl.BlockSpec(shape, lambda c: (0,) * len(shape))
    return pl.pallas_call(
        functools.partial(_ssm_kernel, B=B, Lc=Lc),
        grid=(S // Lc,),
        in_specs=[pl.BlockSpec((B, Lc, SSM_WIDTH), lambda c: (0, c, OFF_US // SSM_WIDTH)),
                  full(*wbr.shape), full(*wbi.shape), full(*ar.shape), full(*ai.shape),
                  full(*cr.shape), full(*ci.shape), full(1, SSM_WIDTH),
                  pl.BlockSpec((None,) + w_glu.shape[1:], lambda c: (l, 0, 0))],
        out_specs=pl.BlockSpec((B, Lc, SSM_WIDTH), lambda c: (0, c, 0)),
        out_shape=jax.ShapeDtypeStruct((B, S, SSM_WIDTH), jnp.bfloat16),
        scratch_shapes=[pltpu.VMEM((SSM_STATE_TILES, B * Lc, LANES), jnp.float32),
                        pltpu.VMEM((SSM_STATE_TILES, B * Lc, LANES), jnp.float32),
                        pltpu.VMEM((SSM_STATE_TILES, B, LANES), jnp.float32),
                        pltpu.VMEM((SSM_STATE_TILES, B, LANES), jnp.float32)],
        compiler_params=_params("arbitrary"),
        name="s5_ssm",
    )(proj.reshape(B, S, n_in), wbr, wbi, ar, ai, cr, ci, d_skip.reshape(1, SSM_WIDTH), w_glu)


def _sb_kernel(q_ref, k_ref, v_ref, o_ref, *, tq, heads):
    qi = pl.program_id(2)
    hd = HEAD_DIM_SB
    row = lax.broadcasted_iota(jnp.int32, (tq, tq), 0)
    col = lax.broadcasted_iota(jnp.int32, (tq, tq), 1)
    after = jnp.where(row > col, 1.0, 0.0).astype(jnp.bfloat16)
    after2 = jnp.concatenate([after, after], axis=0)
    causal = col < row
    dn = (((1,), (1,)), ((), ()))
    scale = hd ** -0.5
    log2e = math.log2(math.e)
    ln2 = math.log(2.0)

    def blocks(j, state, diagonal):
        start = pl.multiple_of(j * tq, tq)
        zs = []
        for h in range(heads):
            k = k_ref[pl.ds(start, tq), h * hd:(h + 1) * hd]
            zs.append(lax.dot_general(qs[h], k, dn, preferred_element_type=jnp.float32) * scale)
        betweens, log_betas, sums = [], [], []
        for h in range(heads):
            z = zs[h]
            sp = jnp.maximum(z, 0.0) + jnp.log(1.0 + jnp.exp2(jnp.abs(z) * (-log2e)))
            log_betas.append(z - sp)
            if diagonal:
                sp = jnp.where(causal, sp, 0.0)
            hi = sp.astype(jnp.bfloat16)
            lo = (sp - hi.astype(jnp.float32)).astype(jnp.bfloat16)
            betweens.append(jnp.dot(jnp.concatenate([hi, lo], axis=1), after2,
                                    preferred_element_type=jnp.float32))
            sums.append(jnp.sum(sp, axis=-1, keepdims=True))
        out = []
        for h in range(heads):
            carry, acc = state[2 * h], state[2 * h + 1]
            w = jnp.exp2((log_betas[h] - betweens[h] - carry) * log2e)
            if diagonal:
                w = jnp.where(causal, w, 0.0)
            v = v_ref[pl.ds(start, tq), h * hd:(h + 1) * hd]
            acc = acc + jnp.dot(w.astype(jnp.bfloat16), v, preferred_element_type=jnp.float32)
            out.extend((carry + sums[h], acc))
        return tuple(out)

    qs = [q_ref[:, h * hd:(h + 1) * hd] for h in range(heads)]
    state = blocks(qi, (jnp.zeros((tq, 1), jnp.float32), jnp.zeros((tq, hd), jnp.float32)) * heads, True)

    def min_carry(state):
        m = state[0]
        for h in range(1, heads):
            m = jnp.minimum(m, state[2 * h])
        return jnp.min(m)

    def cond(loop):
        return jnp.logical_and(loop[0] < qi, loop[1] < SB_EXP_UNDERFLOW)

    def body(loop):
        state = blocks(qi - 1 - loop[0], loop[2:], False)
        return (loop[0] + 1, min_carry(state)) + state

    state = lax.while_loop(cond, body, (jnp.int32(0), min_carry(state)) + state)[2:]
    for h in range(heads):
        o_ref[:, h * hd:(h + 1) * hd] = state[2 * h + 1].astype(o_ref.dtype)


def _stick_breaking(proj, B, S):
    tq = _tile(S, 256, LANES)
    nq = S // tq
    heads = 4
    hd = heads * HEAD_DIM_SB
    return pl.pallas_call(
        functools.partial(_sb_kernel, tq=tq, heads=heads),
        grid=(B, N_SB // heads, nq),
        in_specs=[pl.BlockSpec((tq, hd), lambda b, h, i: (b * nq + i, OFF_QS // hd + h)),
                  pl.BlockSpec((S, hd), lambda b, h, i: (b, OFF_KS // hd + h)),
                  pl.BlockSpec((S, hd), lambda b, h, i: (b, OFF_VS // hd + h))],
        out_specs=pl.BlockSpec((tq, hd), lambda b, h, i: (b * nq + i, h)),
        out_shape=jax.ShapeDtypeStruct((B * S, SB_W), jnp.bfloat16),
        compiler_params=_params("parallel", "parallel", "arbitrary"),
        name="stick_breaking",
    )(proj, proj, proj)


def _merge_kernel(ya_ref, ys_ref, yc_ref, w_ref, ga_ref, gs_ref, gc_ref, o_ref):
    acc = None
    for k, (y_ref, g_ref) in enumerate(((ya_ref, ga_ref), (ys_ref, gs_ref), (yc_ref, gc_ref))):
        t = jnp.dot(y_ref[...], w_ref[k], preferred_element_type=jnp.float32)
        t = _sigmoid(g_ref[...].astype(jnp.float32)) * t
        acc = t if acc is None else acc + t
    o_ref[...] = acc.astype(o_ref.dtype)


def _merge(y_a, y_s, y_c, w_branch, l, proj, D):
    M = y_a.shape[0]
    tm = _tile(M, 1024, BF16_SUBLANES)
    tn = _tile(math.gcd(D, OFF_G), 512, LANES)
    gate = lambda k: (lambda m, n: (m, (OFF_G + k * D) // tn + n))
    row = pl.BlockSpec((tm, BRANCH_WIDTH), lambda m, n: (m, 0))
    return pl.pallas_call(
        _merge_kernel,
        grid=(M // tm, D // tn),
        in_specs=[row, row, row,
                  pl.BlockSpec((None, N_BRANCH, BRANCH_WIDTH, tn), lambda m, n: (l, 0, 0, n)),
                  pl.BlockSpec((tm, tn), gate(0)),
                  pl.BlockSpec((tm, tn), gate(1)),
                  pl.BlockSpec((tm, tn), gate(2))],
        out_specs=pl.BlockSpec((tm, tn), lambda m, n: (m, n)),
        out_shape=jax.ShapeDtypeStruct((M, D), jnp.bfloat16),
        compiler_params=_params("parallel", "arbitrary"),
        name="gated_merge",
    )(y_a, y_s, y_c, w_branch, proj, proj, proj)


def _ffn_up_kernel(h_ref, halo_ref, wa_ref, wb_ref, cwa_ref, cwb_ref, cba_ref, cbb_ref,
                   o_ref, a_scr, *, tm, tiles_per_seq):
    H = BF16_SUBLANES

    @pl.when(pl.program_id(1) == 0)
    def _():
        first = pl.program_id(0) % tiles_per_seq == 0
        halo = halo_ref[...]
        a_scr[0:H, :] = jnp.where(first, jnp.zeros_like(halo), halo)
        a_scr[H:, :] = h_ref[...]

    a = a_scr[...]

    def conv(w_ref, cw_ref, cb_ref):
        u = jnp.dot(a, w_ref[...], preferred_element_type=jnp.float32)
        u1 = pltpu.roll(u, 1, 0)
        u2 = pltpu.roll(u, 2, 0)
        return (cb_ref[...] + cw_ref[0:1, :] * u2[H:] + cw_ref[1:2, :] * u1[H:]
                + cw_ref[2:3, :] * u[H:])

    ga = conv(wa_ref, cwa_ref, cba_ref)
    gb = conv(wb_ref, cwb_ref, cbb_ref)
    o_ref[...] = (_gelu_tanh(ga) * gb).astype(o_ref.dtype)


def _ffn_up(h, w_up, l, conv_w, conv_b, S):
    M, D = h.shape
    F = w_up.shape[2] // 2
    H = BF16_SUBLANES
    tm = _tile(S, 1024, H)
    tn = _tile(F, 512, LANES)
    nf = F // tn
    return pl.pallas_call(
        functools.partial(_ffn_up_kernel, tm=tm, tiles_per_seq=S // tm),
        grid=(M // tm, nf),
        in_specs=[pl.BlockSpec((tm, D), lambda m, n: (m, 0)),
                  pl.BlockSpec((H, D), lambda m, n: (jnp.maximum(m * (tm // H) - 1, 0), 0)),
                  pl.BlockSpec((None, D, tn), lambda m, n: (l, 0, n)),
                  pl.BlockSpec((None, D, tn), lambda m, n: (l, 0, nf + n)),
                  pl.BlockSpec((CONV_W, tn), lambda m, n: (0, n)),
                  pl.BlockSpec((CONV_W, tn), lambda m, n: (0, nf + n)),
                  pl.BlockSpec((1, tn), lambda m, n: (0, n)),
                  pl.BlockSpec((1, tn), lambda m, n: (0, nf + n))],
        out_specs=pl.BlockSpec((tm, tn), lambda m, n: (m, n)),
        out_shape=jax.ShapeDtypeStruct((M, F), jnp.bfloat16),
        scratch_shapes=[pltpu.VMEM((H + tm, D), jnp.bfloat16)],
        compiler_params=_params("parallel", "arbitrary"),
        name="ffn_up_conv_gate",
    )(h, h, w_up, w_up, conv_w, conv_w, conv_b.reshape(1, 2 * F), conv_b.reshape(1, 2 * F))


def _reorder_w_in(w, D):
    o = 0
    parts = {}
    for name, width in (("qa", Q_A), ("ka", KV_A), ("va", KV_A), ("us", SSM_WIDTH), ("qs", SB_W),
                        ("ks", SB_W), ("vs", SB_W), ("g", N_BRANCH * D)):
        parts[name] = w[..., o:o + width].astype(jnp.bfloat16)
        o += width
    return jnp.concatenate([parts[n] for n in ("qa", "us", "qs", "ks", "vs", "g", "ka", "va")], axis=-1)


def kernel(x, c, w_ada, b_ada, ada_table, norm_mix_pre, norm_mix_post, norm_ffn_pre, norm_ffn_post,
           w_in, attn_sinks, ssm_a_re, ssm_a_im, ssm_log_dt, ssm_b_re, ssm_b_im, ssm_c_re, ssm_c_im,
           ssm_d, ssm_w_glu, w_branch, w_out, ffn_w_up, ffn_conv_w, ffn_conv_b, ffn_w_down):
    B, S, D = x.shape
    depth = w_in.shape[0]
    T = B * S
    bf16 = jnp.bfloat16
    off_ka = OFF_G + N_BRANCH * D
    off_va = off_ka + KV_A

    w_in_b = _reorder_w_in(w_in, D)
    w_glu_b = ssm_w_glu.astype(bf16)
    w_branch_b = w_branch.astype(bf16)
    w_out_b = w_out.astype(bf16)
    w_up_b = ffn_w_up.astype(bf16)
    w_down_b = ffn_w_down.astype(bf16)

    mod_shared = _ada_proj(c, w_ada, b_ada)
    h = _normmod(x, norm_mix_pre[0], mod_shared, ada_table[0], 0)
    for l in range(depth):
        tab = ada_table[l]
        proj = _matmul(h.reshape(T, D), w_in_b, l, bf16, 1024, 768, "in_proj")
        y_a = _swa(proj, attn_sinks[l], B, S, off_ka, off_va)
        tables = _ssm_tables(ssm_a_re[l], ssm_a_im[l], ssm_log_dt[l], ssm_b_re[l], ssm_b_im[l],
                             ssm_c_re[l], ssm_c_im[l])
        y_s = _ssm(proj, tables, ssm_d[l], w_glu_b, l, B, S).reshape(T, SSM_WIDTH)
        y_c = _stick_breaking(proj, B, S)
        merged = _merge(y_a, y_s, y_c, w_branch_b, l, proj, D)
        y = _matmul(merged, w_out_b, l, bf16, 1024, 1024, "out_proj")
        x, h = _postnorm_residual(x, y, norm_mix_post[l], mod_shared, tab, 2,
                                  next_norm=(norm_ffn_pre[l], tab, 3))

        gact = _ffn_up(h.reshape(T, D), w_up_b, l, ffn_conv_w[l], ffn_conv_b[l], S)
        y = _matmul(gact, w_down_b, l, bf16, 1024, 512, "ffn_down")
        if l + 1 < depth:
            x, h = _postnorm_residual(x, y, norm_ffn_post[l], mod_shared, tab, 5,
                                      next_norm=(norm_mix_pre[l + 1], ada_table[l + 1], 0))
        else:
            x = _postnorm_residual(x, y, norm_ffn_post[l], mod_shared, tab, 5)
    return x
```

```python
import functools
import math

import jax
import jax.numpy as jnp
from jax import lax
from jax.experimental import pallas as pl
from jax.experimental.pallas import tpu as pltpu

HEAD_DIM_A = 64
N_Q_A = 16
N_KV_A = 2
WINDOW = 128
SSM_WIDTH = 1024
SSM_GROUP = 16
SSM_GROUPS = SSM_WIDTH // SSM_GROUP
SSM_STATE = 64
N_SB = 8
HEAD_DIM_SB = 128
BRANCH_WIDTH = 1024
N_BRANCH = 3
CONV_W = 3
N_MOD = 6
EPS = 1e-6
NEG_INF = -1e30

SB_EXP_UNDERFLOW = 150.0

Q_A = N_Q_A * HEAD_DIM_A
KV_A = N_KV_A * HEAD_DIM_A
SB_W = N_SB * HEAD_DIM_SB

OFF_QA = 0
OFF_KA = OFF_QA + Q_A
OFF_VA = OFF_KA + KV_A
OFF_US = OFF_VA + KV_A
OFF_QS = OFF_US + SSM_WIDTH
OFF_KS = OFF_QS + SB_W
OFF_VS = OFF_KS + SB_W
OFF_G = OFF_VS + SB_W
COL_BLOCK = 256

LANES = 128
BF16_SUBLANES = 16
VMEM_LIMIT_BYTES = 56 * 1024 * 1024

SSM_GROUPS_PER_BLOCK = LANES // SSM_GROUP
SSM_BLOCKS = SSM_GROUPS // SSM_GROUPS_PER_BLOCK
SSM_STATES_PER_BLOCK = SSM_GROUPS_PER_BLOCK * SSM_STATE
SSM_STATES = SSM_GROUPS * SSM_STATE
SSM_STATE_TILES = SSM_STATES // LANES
SSM_PACK = 2


def _tile(dim, pref, align):
    t = (min(pref, dim) // align) * align
    while t >= align:
        if dim % t == 0:
            return t
        t -= align
    return dim


def _params(*sem):
    return pltpu.CompilerParams(dimension_semantics=sem, vmem_limit_bytes=VMEM_LIMIT_BYTES)


def _gelu_tanh(x):
    return 0.5 * x * (1.0 + jnp.tanh(math.sqrt(2.0 / math.pi) * (x + 0.044715 * (x * x * x))))


def _sigmoid(x):
    return 1.0 / (1.0 + jnp.exp(-x))


def _ada_kernel(c_ref, w_ref, b_ref, o_ref):
    c = c_ref[...]
    a = (c * _sigmoid(c)).astype(jnp.bfloat16)
    o_ref[...] = jnp.dot(a, w_ref[...].astype(jnp.bfloat16),
                         preferred_element_type=jnp.float32) + b_ref[...]


def _ada_proj(c, w_ada, b_ada):
    B, D = c.shape
    N = w_ada.shape[1]
    rows = 8
    c_pad = jnp.zeros((rows, D), jnp.float32).at[:B].set(c)
    tn = _tile(N, 512, LANES)
    out = pl.pallas_call(
        _ada_kernel,
        grid=(N // tn,),
        in_specs=[pl.BlockSpec((rows, D), lambda n: (0, 0)),
                  pl.BlockSpec((D, tn), lambda n: (0, n)),
                  pl.BlockSpec((1, tn), lambda n: (0, n))],
        out_specs=pl.BlockSpec((rows, tn), lambda n: (0, n)),
        out_shape=jax.ShapeDtypeStruct((rows, N), jnp.float32),
        compiler_params=_params("parallel"),
        name="ada_proj",
    )(c_pad, w_ada, b_ada.reshape(1, N))
    return out[:B].reshape(B, N_MOD, D)


def _normmod_kernel(x_ref, g_ref, mod_ref, tab_ref, h_ref, *, shift_idx):
    x = x_ref[0]
    ms = jnp.mean(x * x, axis=-1, keepdims=True)
    y = x * lax.rsqrt(ms + EPS) * g_ref[...]
    shift = mod_ref[0, shift_idx:shift_idx + 1, :] + tab_ref[shift_idx:shift_idx + 1, :]
    scale = mod_ref[0, shift_idx + 1:shift_idx + 2, :] + tab_ref[shift_idx + 1:shift_idx + 2, :]
    h_ref[0] = (y * (1.0 + scale) + shift).astype(h_ref.dtype)


def _normmod(x, g, mod_shared, tab, shift_idx):
    B, S, D = x.shape
    tr = _tile(S, 256, BF16_SUBLANES)
    return pl.pallas_call(
        functools.partial(_normmod_kernel, shift_idx=shift_idx),
        grid=(B, S // tr),
        in_specs=[pl.BlockSpec((1, tr, D), lambda b, i: (b, i, 0)),
                  pl.BlockSpec((1, D), lambda b, i: (0, 0)),
                  pl.BlockSpec((1, N_MOD, D), lambda b, i: (b, 0, 0)),
                  pl.BlockSpec((N_MOD, D), lambda b, i: (0, 0))],
        out_specs=pl.BlockSpec((1, tr, D), lambda b, i: (b, i, 0)),
        out_shape=jax.ShapeDtypeStruct((B, S, D), jnp.bfloat16),
        compiler_params=_params("parallel", "parallel"),
        name="normmod",
    )(x, g.reshape(1, D), mod_shared, tab)


def _postnorm_kernel(x_ref, y_ref, g_ref, mod_ref, tab_ref, *rest, gate_idx, next_shift_idx):
    y = y_ref[0].astype(jnp.float32)
    ms = jnp.mean(y * y, axis=-1, keepdims=True)
    yn = y * lax.rsqrt(ms + EPS) * g_ref[...]
    gate = mod_ref[0, gate_idx:gate_idx + 1, :] + tab_ref[gate_idx:gate_idx + 1, :]
    x = x_ref[0] + gate * yn
    if next_shift_idx is None:
        (o_ref,) = rest
        o_ref[0] = x
        return
    gn_ref, tabn_ref, o_ref, h_ref = rest
    o_ref[0] = x
    ms = jnp.mean(x * x, axis=-1, keepdims=True)
    xn = x * lax.rsqrt(ms + EPS) * gn_ref[...]
    k = next_shift_idx
    shift = mod_ref[0, k:k + 1, :] + tabn_ref[k:k + 1, :]
    scale = mod_ref[0, k + 1:k + 2, :] + tabn_ref[k + 1:k + 2, :]
    h_ref[0] = (xn * (1.0 + scale) + shift).astype(h_ref.dtype)


def _postnorm_residual(x, y, g, mod_shared, tab, gate_idx, next_norm=None):
    B, S, D = x.shape
    tr = _tile(S, 256, BF16_SUBLANES)
    row = pl.BlockSpec((1, tr, D), lambda b, i: (b, i, 0))
    vec = pl.BlockSpec((1, D), lambda b, i: (0, 0))
    tabs = pl.BlockSpec((N_MOD, D), lambda b, i: (0, 0))
    in_specs = [row, row, vec, pl.BlockSpec((1, N_MOD, D), lambda b, i: (b, 0, 0)), tabs]
    args = [x, y.reshape(B, S, D), g.reshape(1, D), mod_shared, tab]
    x_shape = jax.ShapeDtypeStruct((B, S, D), jnp.float32)
    if next_norm is None:
        out_specs, out_shape, next_shift_idx = row, x_shape, None
    else:
        g_next, tab_next, next_shift_idx = next_norm
        in_specs += [vec, tabs]
        args += [g_next.reshape(1, D), tab_next]
        out_specs = (row, row)
        out_shape = (x_shape, jax.ShapeDtypeStruct((B, S, D), jnp.bfloat16))
    return pl.pallas_call(
        functools.partial(_postnorm_kernel, gate_idx=gate_idx, next_shift_idx=next_shift_idx),
        grid=(B, S // tr),
        in_specs=in_specs,
        out_specs=out_specs,
        out_shape=out_shape,
        compiler_params=_params("parallel", "parallel"),
        name="postnorm_residual",
    )(*args)


def _mm_kernel(a_ref, w_ref, o_ref):
    o_ref[...] = jnp.dot(a_ref[...], w_ref[...],
                         preferred_element_type=jnp.float32).astype(o_ref.dtype)


def _matmul(a, w, l, out_dtype, tm_pref, tn_pref, name):
    M, K = a.shape
    N = w.shape[2]
    tm = _tile(M, tm_pref, BF16_SUBLANES)
    tn = _tile(N, tn_pref, LANES)
    return pl.pallas_call(
        _mm_kernel,
        grid=(M // tm, N // tn),
        in_specs=[pl.BlockSpec((tm, K), lambda m, n: (m, 0)),
                  pl.BlockSpec((None, K, tn), lambda m, n: (l, 0, n))],
        out_specs=pl.BlockSpec((tm, tn), lambda m, n: (m, n)),
        out_shape=jax.ShapeDtypeStruct((M, N), out_dtype),
        compiler_params=_params("parallel", "arbitrary"),
        name=name,
    )(a, w)


def _swa_kernel(sink_ref, biasp_ref, biasc_ref, q_ref, kp_ref, kc_ref, vp_ref, vc_ref, o_ref):
    i = pl.program_id(1)
    W = WINDOW
    Dh = HEAD_DIM_A
    G = N_Q_A // N_KV_A
    n_pairs = N_Q_A // 2
    dn = (((1,), (1,)), ((), ()))
    zeros = jnp.zeros((W, Dh), jnp.bfloat16)

    def blockdiag(x):
        return jnp.concatenate([jnp.concatenate([x, zeros], axis=1),
                                jnp.concatenate([zeros, x], axis=1)], axis=0)

    kbd_p, kbd_c, vbd_p, vbd_c = [], [], [], []
    for hk in range(N_KV_A):
        sl = slice(hk * Dh, (hk + 1) * Dh)
        kbd_p.append(blockdiag(kp_ref[:, sl]))
        kbd_c.append(blockdiag(kc_ref[:, sl]))
        vbd_p.append(blockdiag(vp_ref[:, sl]))
        vbd_c.append(blockdiag(vc_ref[:, sl]))

    no_prev = jnp.where(i > 0, 0.0, NEG_INF)
    scores = []
    for p in range(n_pairs):
        hk = (2 * p) // G
        q = q_ref[:, p * 2 * Dh:(p + 1) * 2 * Dh]
        scores.append((lax.dot_general(q, kbd_p[hk], dn, preferred_element_type=jnp.float32),
                       lax.dot_general(q, kbd_c[hk], dn, preferred_element_type=jnp.float32)))

    lane = lax.broadcasted_iota(jnp.int32, (W, 2 * Dh), 1)
    for p in range(n_pairs):
        hk = (2 * p) // G
        tp = scores[p][0] * (Dh ** -0.5) + (biasp_ref[p] + no_prev)
        tc = scores[p][1] * (Dh ** -0.5) + biasc_ref[p]
        eps, ecs, invs = [], [], []
        for half in range(2):
            sink = sink_ref[2 * p + half]
            hl = slice(half * W, (half + 1) * W)
            m = jnp.maximum(jnp.max(jnp.maximum(tp[:, hl], tc[:, hl]), axis=-1, keepdims=True), sink)
            ep = jnp.exp(tp[:, hl] - m)
            ec = jnp.exp(tc[:, hl] - m)
            denom = jnp.sum(ep + ec, axis=-1, keepdims=True) + jnp.exp(sink - m)
            eps.append(ep.astype(jnp.bfloat16))
            ecs.append(ec.astype(jnp.bfloat16))
            invs.append(1.0 / denom)
        o = (jnp.dot(jnp.concatenate(eps, axis=1), vbd_p[hk], preferred_element_type=jnp.float32)
             + jnp.dot(jnp.concatenate(ecs, axis=1), vbd_c[hk], preferred_element_type=jnp.float32))
        o = o * jnp.where(lane < Dh, invs[0], invs[1])
        o_ref[:, p * 2 * Dh:(p + 1) * 2 * Dh] = o.astype(o_ref.dtype)


def _swa_bias():
    W = WINDOW
    r = jnp.arange(W)[:, None]
    c = jnp.arange(W)[None, :]
    slopes = jnp.exp2(-8.0 * jnp.arange(1, N_Q_A + 1, dtype=jnp.float32) / N_Q_A)[:, None, None]
    bp = jnp.where(c > r, -slopes * (r + W - c).astype(jnp.float32), NEG_INF)
    bc = jnp.where(c <= r, -slopes * (r - c).astype(jnp.float32), NEG_INF)
    pair = lambda b: b.reshape(N_Q_A // 2, 2, W, W).transpose(0, 2, 1, 3).reshape(N_Q_A // 2, W, 2 * W)
    return pair(bp), pair(bc)


def _swa(proj, sinks, B, S):
    nb = S // WINDOW
    kb = OFF_KA // KV_A
    vb = OFF_VA // KV_A
    cur = lambda col: (lambda b, i: (b * nb + i, col))
    prev = lambda col: (lambda b, i: (b * nb + jnp.maximum(i - 1, 0), col))
    biasp, biasc = _swa_bias()
    bias_spec = pl.BlockSpec(biasp.shape, lambda b, i: (0, 0, 0))
    return pl.pallas_call(
        _swa_kernel,
        grid=(B, nb),
        in_specs=[pl.BlockSpec(memory_space=pltpu.SMEM), bias_spec, bias_spec,
                  pl.BlockSpec((WINDOW, Q_A), cur(OFF_QA // Q_A)),
                  pl.BlockSpec((WINDOW, KV_A), prev(kb)),
                  pl.BlockSpec((WINDOW, KV_A), cur(kb)),
                  pl.BlockSpec((WINDOW, KV_A), prev(vb)),
                  pl.BlockSpec((WINDOW, KV_A), cur(vb))],
        out_specs=pl.BlockSpec((WINDOW, Q_A), lambda b, i: (b * nb + i, 0)),
        out_shape=jax.ShapeDtypeStruct((B * S, Q_A), jnp.bfloat16),
        compiler_params=_params("parallel", "parallel"),
        name="swa",
    )(sinks, biasp, biasc, proj, proj, proj, proj, proj)


def _ssm_kernel(*refs, B, Lc):
    n_u = SSM_WIDTH // COL_BLOCK
    u_refs = refs[:n_u]
    wbr_ref, wbi_ref, a1_ref, a2_ref, cr_ref, ci_ref, d_ref, wg_ref, o_ref, s_ref, x_ref = refs[n_u:]

    @pl.when(pl.program_id(0) == 0)
    def _():
        x_ref[...] = jnp.zeros_like(x_ref)

    R = B * Lc
    u = jnp.concatenate([r[...] for r in u_refs], axis=-1).reshape(R, SSM_WIDTH)
    lpb = SSM_STATES_PER_BLOCK // LANES

    shape = x_ref.shape
    sub = shape[1]

    def time_rows(tile, part, b):
        return pl.ds(((tile % SSM_PACK) * 2 + part) * B + b, Lc, stride=sub)

    for j in range(SSM_BLOCKS):
        uj = u[:, j * LANES:(j + 1) * LANES]
        drive = (jnp.dot(uj, wbr_ref[j], preferred_element_type=jnp.float32),
                 jnp.dot(uj, wbi_ref[j], preferred_element_type=jnp.float32))
        for k in range(lpb):
            tile = j * lpb + k
            for part in range(2):
                for b in range(B):
                    s_ref[tile // SSM_PACK, time_rows(tile, part, b), :] = (
                        drive[part][b * Lc:(b + 1) * Lc, k * LANES:(k + 1) * LANES])

    is_re = lax.broadcasted_iota(jnp.int32, shape, 1) % (2 * B) < B

    def step(t, x):
        rows = pl.ds(pl.multiple_of(t * sub, sub), sub)
        swapped = jnp.where(is_re, pltpu.roll(x, sub - B, 1), pltpu.roll(x, B, 1))
        new = a1_ref[...] * x + a2_ref[...] * swapped + s_ref[:, rows, :]
        s_ref[:, rows, :] = new
        return new

    x_ref[...] = lax.fori_loop(0, Lc, step, x_ref[...], unroll=4)

    def states(j, part):
        return jnp.concatenate(
            [jnp.concatenate([s_ref[t // SSM_PACK, time_rows(t, part, b), :] for b in range(B)], axis=0)
             for t in range(j * lpb, (j + 1) * lpb)], axis=-1).astype(jnp.bfloat16)

    ys = []
    for j in range(SSM_BLOCKS):
        ys.append(jnp.dot(states(j, 0), cr_ref[j], preferred_element_type=jnp.float32)
                  + jnp.dot(states(j, 1), ci_ref[j], preferred_element_type=jnp.float32))
    y = jnp.concatenate(ys, axis=-1) + d_ref[...] * u.astype(jnp.float32)
    y = _gelu_tanh(y).astype(jnp.bfloat16)
    vg = jnp.dot(y, wg_ref[...], preferred_element_type=jnp.float32)
    out = vg[:, :SSM_WIDTH] * _sigmoid(vg[:, SSM_WIDTH:])
    o_ref[...] = out.reshape(B, Lc, SSM_WIDTH).astype(o_ref.dtype)


def _ssm_tables(a_re, a_im, log_dt, b_re, b_im, c_re, c_im, batch):
    f32 = jnp.float32
    lr, li = a_re.astype(f32), a_im.astype(f32)
    dt = jnp.exp(log_dt.astype(f32))[:, None]
    mag = jnp.exp(lr * dt)
    abr, abi = mag * jnp.cos(li * dt), mag * jnp.sin(li * dt)
    den = lr * lr + li * li
    kr = ((abr - 1.0) * lr + abi * li) / den
    ki = (abi * lr - (abr - 1.0) * li) / den
    br, bi = b_re.astype(f32), b_im.astype(f32)
    bbr = kr[..., None] * br - ki[..., None] * bi
    bbi = kr[..., None] * bi + ki[..., None] * br
    gpb, P, N = SSM_GROUPS_PER_BLOCK, SSM_GROUP, SSM_STATE
    eye = jnp.eye(gpb, dtype=f32)

    def in_blocks(w):
        w = w.reshape(SSM_BLOCKS, gpb, N, P)
        return jnp.einsum('jgnp,gh->jgphn', w, eye).reshape(SSM_BLOCKS, gpb * P, gpb * N)

    def out_blocks(w):
        w = w.reshape(SSM_BLOCKS, gpb, P, N)
        return jnp.einsum('jgpn,gh->jgnhp', w, eye).reshape(SSM_BLOCKS, gpb * N, gpb * P)

    wbr = in_blocks(bbr).astype(jnp.bfloat16)
    wbi = in_blocks(bbi).astype(jnp.bfloat16)
    cr = out_blocks(c_re.astype(f32)).astype(jnp.bfloat16)
    ci = out_blocks(-c_im.astype(f32)).astype(jnp.bfloat16)
    pairs = SSM_STATE_TILES // SSM_PACK
    a1 = jnp.broadcast_to(abr.reshape(pairs, SSM_PACK, 1, 1, LANES), (pairs, SSM_PACK, 2, batch, LANES))
    sign = jnp.array([-1.0, 1.0], f32).reshape(1, 1, 2, 1, 1)
    a2 = jnp.broadcast_to(abi.reshape(pairs, SSM_PACK, 1, 1, LANES) * sign, (pairs, SSM_PACK, 2, batch, LANES))
    rows = SSM_PACK * 2 * batch
    return wbr, wbi, a1.reshape(pairs, rows, LANES), a2.reshape(pairs, rows, LANES), cr, ci


def _ssm(proj, tables, d_skip, w_glu, l, B, S):
    wbr, wbi, a1, a2, cr, ci = tables
    n_in = proj.shape[1]
    Lc = _tile(S, 256, BF16_SUBLANES)
    rows = SSM_PACK * 2 * B
    assert rows == 8, "packed scan layout needs (lane tiles per vreg) * 2 * batch == 8 sublanes"
    pairs = SSM_STATE_TILES // SSM_PACK
    full = lambda *shape: pl.BlockSpec(shape, lambda c: (0,) * len(shape))
    n_u = SSM_WIDTH // COL_BLOCK
    u_specs = [pl.BlockSpec((B, Lc, COL_BLOCK), functools.partial(lambda c, s: (0, c, OFF_US // COL_BLOCK + s), s=s))
               for s in range(n_u)]
    proj3 = proj.reshape(B, S, n_in)
    return pl.pallas_call(
        functools.partial(_ssm_kernel, B=B, Lc=Lc),
        grid=(S // Lc,),
        in_specs=u_specs + [full(*wbr.shape), full(*wbi.shape), full(*a1.shape), full(*a2.shape),
                            full(*cr.shape), full(*ci.shape), full(1, SSM_WIDTH),
                            pl.BlockSpec((None,) + w_glu.shape[1:], lambda c: (l, 0, 0))],
        out_specs=pl.BlockSpec((B, Lc, SSM_WIDTH), lambda c: (0, c, 0)),
        out_shape=jax.ShapeDtypeStruct((B, S, SSM_WIDTH), jnp.bfloat16),
        scratch_shapes=[pltpu.VMEM((pairs, rows * Lc, LANES), jnp.float32),
                        pltpu.VMEM((pairs, rows, LANES), jnp.float32)],
        compiler_params=_params("arbitrary"),
        name="s5_ssm",
    )(*([proj3] * n_u), wbr, wbi, a1, a2, cr, ci, d_skip.reshape(1, SSM_WIDTH), w_glu)


def _sb_kernel(*refs, tq, heads):
    q_refs, k_refs, v_refs = refs[:heads], refs[heads:2 * heads], refs[2 * heads:3 * heads]
    o_ref = refs[3 * heads]
    qi = pl.program_id(2)
    hd = HEAD_DIM_SB
    row = lax.broadcasted_iota(jnp.int32, (tq, tq), 0)
    col = lax.broadcasted_iota(jnp.int32, (tq, tq), 1)
    after = jnp.where(row > col, 1.0, 0.0).astype(jnp.bfloat16)
    after2 = jnp.concatenate([after, after], axis=0)
    causal = col < row
    dn = (((1,), (1,)), ((), ()))
    scale = hd ** -0.5
    log2e = math.log2(math.e)
    ln2 = math.log(2.0)

    def blocks(j, state, diagonal):
        start = pl.multiple_of(j * tq, tq)
        zs = []
        for h in range(heads):
            k = k_refs[h][pl.ds(start, tq), :]
            zs.append(lax.dot_general(qs[h], k, dn, preferred_element_type=jnp.float32) * scale)
        betweens, log_betas, sums = [], [], []
        for h in range(heads):
            z = zs[h]
            sp = jnp.maximum(z, 0.0) + jnp.log(1.0 + jnp.exp2(jnp.abs(z) * (-log2e)))
            log_betas.append(z - sp)
            if diagonal:
                sp = jnp.where(causal, sp, 0.0)
            hi = sp.astype(jnp.bfloat16)
            lo = (sp - hi.astype(jnp.float32)).astype(jnp.bfloat16)
            betweens.append(jnp.dot(jnp.concatenate([hi, lo], axis=1), after2,
                                    preferred_element_type=jnp.float32))
            sums.append(jnp.sum(sp, axis=-1, keepdims=True))
        out = []
        for h in range(heads):
            carry, acc = state[2 * h], state[2 * h + 1]
            w = jnp.exp2((log_betas[h] - betweens[h] - carry) * log2e)
            if diagonal:
                w = jnp.where(causal, w, 0.0)
            v = v_refs[h][pl.ds(start, tq), :]
            acc = acc + jnp.dot(w.astype(jnp.bfloat16), v, preferred_element_type=jnp.float32)
            out.extend((carry + sums[h], acc))
        return tuple(out)

    qs = [q_refs[h][...] for h in range(heads)]
    state = blocks(qi, (jnp.zeros((tq, 1), jnp.float32), jnp.zeros((tq, hd), jnp.float32)) * heads, True)

    def min_carry(state):
        m = state[0]
        for h in range(1, heads):
            m = jnp.minimum(m, state[2 * h])
        return jnp.min(m)

    def cond(loop):
        return jnp.logical_and(loop[0] < qi, loop[1] < SB_EXP_UNDERFLOW)

    def body(loop):
        state = blocks(qi - 1 - loop[0], loop[2:], False)
        return (loop[0] + 1, min_carry(state)) + state

    state = lax.while_loop(cond, body, (jnp.int32(0), min_carry(state)) + state)[2:]
    for h in range(heads):
        o_ref[:, h * hd:(h + 1) * hd] = state[2 * h + 1].astype(o_ref.dtype)


def _stick_breaking(proj, B, S):
    tq = _tile(S, 256, LANES)
    nq = S // tq
    heads = 4
    hd = HEAD_DIM_SB

    def head_specs(rows, off, row_index):
        return [pl.BlockSpec((rows, hd), functools.partial(
                    lambda b, g, i, s: (row_index(b, i), off // hd + g * heads + s), s=s))
                for s in range(heads)]

    return pl.pallas_call(
        functools.partial(_sb_kernel, tq=tq, heads=heads),
        grid=(B, N_SB // heads, nq),
        in_specs=(head_specs(tq, OFF_QS, lambda b, i: b * nq + i)
                  + head_specs(S, OFF_KS, lambda b, i: b)
                  + head_specs(S, OFF_VS, lambda b, i: b)),
        out_specs=pl.BlockSpec((tq, heads * hd), lambda b, g, i: (b * nq + i, g)),
        out_shape=jax.ShapeDtypeStruct((B * S, SB_W), jnp.bfloat16),
        compiler_params=_params("parallel", "parallel", "arbitrary"),
        name="stick_breaking",
    )(*([proj] * (3 * heads)))


def _merge_kernel(ya_ref, ys_ref, yc_ref, w_ref, *rest):
    o_ref = rest[-1]
    g_refs = rest[:-1]
    subs = len(g_refs) // N_BRANCH
    acc = None
    for k, y_ref in enumerate((ya_ref, ys_ref, yc_ref)):
        t = jnp.dot(y_ref[...], w_ref[k], preferred_element_type=jnp.float32)
        g = jnp.concatenate([g_refs[k * subs + s][...] for s in range(subs)], axis=-1)
        t = _sigmoid(g.astype(jnp.float32)) * t
        acc = t if acc is None else acc + t
    o_ref[...] = acc.astype(o_ref.dtype)


def _merge(y_a, y_s, y_c, w_branch, l, proj, D):
    M = y_a.shape[0]
    tm = _tile(M, 1024, BF16_SUBLANES)
    tn = _tile(D, 512, COL_BLOCK)
    subs = tn // COL_BLOCK
    gate_specs = [pl.BlockSpec((tm, COL_BLOCK), functools.partial(
                      lambda m, n, k, s: (m, (OFF_G + k * D) // COL_BLOCK + n * subs + s), k=k, s=s))
                  for k in range(N_BRANCH) for s in range(subs)]
    row = pl.BlockSpec((tm, BRANCH_WIDTH), lambda m, n: (m, 0))
    return pl.pallas_call(
        _merge_kernel,
        grid=(M // tm, D // tn),
        in_specs=[row, row, row,
                  pl.BlockSpec((None, N_BRANCH, BRANCH_WIDTH, tn), lambda m, n: (l, 0, 0, n))] + gate_specs,
        out_specs=pl.BlockSpec((tm, tn), lambda m, n: (m, n)),
        out_shape=jax.ShapeDtypeStruct((M, D), jnp.bfloat16),
        compiler_params=_params("parallel", "arbitrary"),
        name="gated_merge",
    )(y_a, y_s, y_c, w_branch, *([proj] * len(gate_specs)))


def _ffn_up_kernel(h_ref, halo_ref, wa_ref, wb_ref, cwa_ref, cwb_ref, cba_ref, cbb_ref,
                   o_ref, a_scr, *, tm, tiles_per_seq):
    H = BF16_SUBLANES

    @pl.when(pl.program_id(1) == 0)
    def _():
        first = pl.program_id(0) % tiles_per_seq == 0
        halo = halo_ref[...]
        a_scr[0:H, :] = jnp.where(first, jnp.zeros_like(halo), halo)
        a_scr[H:, :] = h_ref[...]

    a = a_scr[...]

    def conv(w_ref, cw_ref, cb_ref):
        u = jnp.dot(a, w_ref[...], preferred_element_type=jnp.float32)
        u1 = pltpu.roll(u, 1, 0)
        u2 = pltpu.roll(u, 2, 0)
        return (cb_ref[...] + cw_ref[0:1, :] * u2[H:] + cw_ref[1:2, :] * u1[H:]
                + cw_ref[2:3, :] * u[H:])

    ga = conv(wa_ref, cwa_ref, cba_ref)
    gb = conv(wb_ref, cwb_ref, cbb_ref)
    o_ref[...] = (_gelu_tanh(ga) * gb).astype(o_ref.dtype)


def _ffn_up(h, w_up, l, conv_w, conv_b, S):
    M, D = h.shape
    F = w_up.shape[2] // 2
    H = BF16_SUBLANES
    tm = _tile(S, 1024, H)
    tn = _tile(F, 512, LANES)
    nf = F // tn
    return pl.pallas_call(
        functools.partial(_ffn_up_kernel, tm=tm, tiles_per_seq=S // tm),
        grid=(M // tm, nf),
        in_specs=[pl.BlockSpec((tm, D), lambda m, n: (m, 0)),
                  pl.BlockSpec((H, D), lambda m, n: (jnp.maximum(m * (tm // H) - 1, 0), 0)),
                  pl.BlockSpec((None, D, tn), lambda m, n: (l, 0, n)),
                  pl.BlockSpec((None, D, tn), lambda m, n: (l, 0, nf + n)),
                  pl.BlockSpec((CONV_W, tn), lambda m, n: (0, n)),
                  pl.BlockSpec((CONV_W, tn), lambda m, n: (0, nf + n)),
                  pl.BlockSpec((1, tn), lambda m, n: (0, n)),
                  pl.BlockSpec((1, tn), lambda m, n: (0, nf + n))],
        out_specs=pl.BlockSpec((tm, tn), lambda m, n: (m, n)),
        out_shape=jax.ShapeDtypeStruct((M, F), jnp.bfloat16),
        scratch_shapes=[pltpu.VMEM((H + tm, D), jnp.bfloat16)],
        compiler_params=_params("parallel", "arbitrary"),
        name="ffn_up_conv_gate",
    )(h, h, w_up, w_up, conv_w, conv_w, conv_b.reshape(1, 2 * F), conv_b.reshape(1, 2 * F))


def kernel(x, c, w_ada, b_ada, ada_table, norm_mix_pre, norm_mix_post, norm_ffn_pre, norm_ffn_post,
           w_in, attn_sinks, ssm_a_re, ssm_a_im, ssm_log_dt, ssm_b_re, ssm_b_im, ssm_c_re, ssm_c_im,
           ssm_d, ssm_w_glu, w_branch, w_out, ffn_w_up, ffn_conv_w, ffn_conv_b, ffn_w_down):
    B, S, D = x.shape
    depth = w_in.shape[0]
    T = B * S
    bf16 = jnp.bfloat16

    w_in_b = w_in.astype(bf16)
    w_glu_b = ssm_w_glu.astype(bf16)
    w_branch_b = w_branch.astype(bf16)
    w_out_b = w_out.astype(bf16)
    w_up_b = ffn_w_up.astype(bf16)
    w_down_b = ffn_w_down.astype(bf16)

    mod_shared = _ada_proj(c, w_ada, b_ada)
    h = _normmod(x, norm_mix_pre[0], mod_shared, ada_table[0], 0)
    for l in range(depth):
        tab = ada_table[l]
        proj = _matmul(h.reshape(T, D), w_in_b, l, bf16, 1024, 768, "in_proj")
        y_a = _swa(proj, attn_sinks[l], B, S)
        tables = _ssm_tables(ssm_a_re[l], ssm_a_im[l], ssm_log_dt[l], ssm_b_re[l], ssm_b_im[l],
                             ssm_c_re[l], ssm_c_im[l], B)
        y_s = _ssm(proj, tables, ssm_d[l], w_glu_b, l, B, S).reshape(T, SSM_WIDTH)
        y_c = _stick_breaking(proj, B, S)
        merged = _merge(y_a, y_s, y_c, w_branch_b, l, proj, D)
        y = _matmul(merged, w_out_b, l, bf16, 1024, 1024, "out_proj")
        x, h = _postnorm_residual(x, y, norm_mix_post[l], mod_shared, tab, 2,
                                  next_norm=(norm_ffn_pre[l], tab, 3))

        gact = _ffn_up(h.reshape(T, D), w_up_b, l, ffn_conv_w[l], ffn_conv_b[l], S)
        y = _matmul(gact, w_down_b, l, bf16, 1024, 512, "ffn_down")
        if l + 1 < depth:
            x, h = _postnorm_residual(x, y, norm_ffn_post[l], mod_shared, tab, 5,
                                      next_norm=(norm_mix_pre[l + 1], ada_table[l + 1], 0))
        else:
            x = _postnorm_residual(x, y, norm_ffn_post[l], mod_shared, tab, 5)
    return x
```

```python
import functools
import math

import jax
import jax.numpy as jnp
from jax import lax
from jax.experimental import pallas as pl
from jax.experimental.pallas import tpu as pltpu

HEAD_DIM_A = 64
N_Q_A = 16
N_KV_A = 2
WINDOW = 128
SSM_WIDTH = 1024
SSM_GROUP = 16
SSM_GROUPS = SSM_WIDTH // SSM_GROUP
SSM_STATE = 64
N_SB = 8
HEAD_DIM_SB = 128
BRANCH_WIDTH = 1024
N_BRANCH = 3
CONV_W = 3
N_MOD = 6
EPS = 1e-6
NEG_INF = -1e30

SB_EXP_UNDERFLOW = 150.0

Q_A = N_Q_A * HEAD_DIM_A
KV_A = N_KV_A * HEAD_DIM_A
SB_W = N_SB * HEAD_DIM_SB

OFF_QA = 0
OFF_KA = OFF_QA + Q_A
OFF_VA = OFF_KA + KV_A
OFF_US = OFF_VA + KV_A
OFF_QS = OFF_US + SSM_WIDTH
OFF_KS = OFF_QS + SB_W
OFF_VS = OFF_KS + SB_W
OFF_G = OFF_VS + SB_W
COL_BLOCK = 256

LANES = 128
BF16_SUBLANES = 16
VMEM_LIMIT_BYTES = 56 * 1024 * 1024

SSM_GROUPS_PER_BLOCK = LANES // SSM_GROUP
SSM_BLOCKS = SSM_GROUPS // SSM_GROUPS_PER_BLOCK
SSM_STATES_PER_BLOCK = SSM_GROUPS_PER_BLOCK * SSM_STATE
SSM_STATES = SSM_GROUPS * SSM_STATE
SSM_STATE_TILES = SSM_STATES // LANES
SSM_PACK = 2


def _tile(dim, pref, align):
    t = (min(pref, dim) // align) * align
    while t >= align:
        if dim % t == 0:
            return t
        t -= align
    return dim


def _params(*sem):
    return pltpu.CompilerParams(dimension_semantics=sem, vmem_limit_bytes=VMEM_LIMIT_BYTES)


def _gelu_tanh(x):
    return 0.5 * x * (1.0 + jnp.tanh(math.sqrt(2.0 / math.pi) * (x + 0.044715 * (x * x * x))))


def _sigmoid(x):
    return 1.0 / (1.0 + jnp.exp(-x))


def _ada_kernel(c_ref, w_ref, b_ref, o_ref):
    c = c_ref[...]
    a = (c * _sigmoid(c)).astype(jnp.bfloat16)
    o_ref[...] = jnp.dot(a, w_ref[...].astype(jnp.bfloat16),
                         preferred_element_type=jnp.float32) + b_ref[...]


def _ada_proj(c, w_ada, b_ada):
    B, D = c.shape
    N = w_ada.shape[1]
    rows = 8
    c_pad = jnp.zeros((rows, D), jnp.float32).at[:B].set(c)
    tn = _tile(N, 512, LANES)
    out = pl.pallas_call(
        _ada_kernel,
        grid=(N // tn,),
        in_specs=[pl.BlockSpec((rows, D), lambda n: (0, 0)),
                  pl.BlockSpec((D, tn), lambda n: (0, n)),
                  pl.BlockSpec((1, tn), lambda n: (0, n))],
        out_specs=pl.BlockSpec((rows, tn), lambda n: (0, n)),
        out_shape=jax.ShapeDtypeStruct((rows, N), jnp.float32),
        compiler_params=_params("parallel"),
        name="ada_proj",
    )(c_pad, w_ada, b_ada.reshape(1, N))
    return out[:B].reshape(B, N_MOD, D)


def _normmod_kernel(x_ref, g_ref, mod_ref, tab_ref, h_ref, *, shift_idx):
    x = x_ref[0]
    ms = jnp.mean(x * x, axis=-1, keepdims=True)
    y = x * lax.rsqrt(ms + EPS) * g_ref[...]
    shift = mod_ref[0, shift_idx:shift_idx + 1, :] + tab_ref[shift_idx:shift_idx + 1, :]
    scale = mod_ref[0, shift_idx + 1:shift_idx + 2, :] + tab_ref[shift_idx + 1:shift_idx + 2, :]
    h_ref[0] = (y * (1.0 + scale) + shift).astype(h_ref.dtype)


def _normmod(x, g, mod_shared, tab, shift_idx):
    B, S, D = x.shape
    tr = _tile(S, 256, BF16_SUBLANES)
    return pl.pallas_call(
        functools.partial(_normmod_kernel, shift_idx=shift_idx),
        grid=(B, S // tr),
        in_specs=[pl.BlockSpec((1, tr, D), lambda b, i: (b, i, 0)),
                  pl.BlockSpec((1, D), lambda b, i: (0, 0)),
                  pl.BlockSpec((1, N_MOD, D), lambda b, i: (b, 0, 0)),
                  pl.BlockSpec((N_MOD, D), lambda b, i: (0, 0))],
        out_specs=pl.BlockSpec((1, tr, D), lambda b, i: (b, i, 0)),
        out_shape=jax.ShapeDtypeStruct((B, S, D), jnp.bfloat16),
        compiler_params=_params("parallel", "parallel"),
        name="normmod",
    )(x, g.reshape(1, D), mod_shared, tab)


def _postnorm_kernel(x_ref, y_ref, g_ref, mod_ref, tab_ref, *rest, gate_idx, next_shift_idx):
    y = y_ref[0].astype(jnp.float32)
    ms = jnp.mean(y * y, axis=-1, keepdims=True)
    yn = y * lax.rsqrt(ms + EPS) * g_ref[...]
    gate = mod_ref[0, gate_idx:gate_idx + 1, :] + tab_ref[gate_idx:gate_idx + 1, :]
    x = x_ref[0] + gate * yn
    if next_shift_idx is None:
        (o_ref,) = rest
        o_ref[0] = x
        return
    gn_ref, tabn_ref, o_ref, h_ref = rest
    o_ref[0] = x
    ms = jnp.mean(x * x, axis=-1, keepdims=True)
    xn = x * lax.rsqrt(ms + EPS) * gn_ref[...]
    k = next_shift_idx
    shift = mod_ref[0, k:k + 1, :] + tabn_ref[k:k + 1, :]
    scale = mod_ref[0, k + 1:k + 2, :] + tabn_ref[k + 1:k + 2, :]
    h_ref[0] = (xn * (1.0 + scale) + shift).astype(h_ref.dtype)


def _postnorm_residual(x, y, g, mod_shared, tab, gate_idx, next_norm=None):
    B, S, D = x.shape
    tr = _tile(S, 256, BF16_SUBLANES)
    row = pl.BlockSpec((1, tr, D), lambda b, i: (b, i, 0))
    vec = pl.BlockSpec((1, D), lambda b, i: (0, 0))
    tabs = pl.BlockSpec((N_MOD, D), lambda b, i: (0, 0))
    in_specs = [row, row, vec, pl.BlockSpec((1, N_MOD, D), lambda b, i: (b, 0, 0)), tabs]
    args = [x, y.reshape(B, S, D), g.reshape(1, D), mod_shared, tab]
    x_shape = jax.ShapeDtypeStruct((B, S, D), jnp.float32)
    if next_norm is None:
        out_specs, out_shape, next_shift_idx = row, x_shape, None
    else:
        g_next, tab_next, next_shift_idx = next_norm
        in_specs += [vec, tabs]
        args += [g_next.reshape(1, D), tab_next]
        out_specs = (row, row)
        out_shape = (x_shape, jax.ShapeDtypeStruct((B, S, D), jnp.bfloat16))
    return pl.pallas_call(
        functools.partial(_postnorm_kernel, gate_idx=gate_idx, next_shift_idx=next_shift_idx),
        grid=(B, S // tr),
        in_specs=in_specs,
        out_specs=out_specs,
        out_shape=out_shape,
        compiler_params=_params("parallel", "parallel"),
        name="postnorm_residual",
    )(*args)


def _mm_kernel(a_ref, w_ref, *rest):
    n_casts = len(rest) // 2
    o_ref = rest[n_casts]
    o_ref[...] = jnp.dot(a_ref[...], w_ref[...],
                         preferred_element_type=jnp.float32).astype(o_ref.dtype)
    for src_ref, dst_ref in zip(rest[:n_casts], rest[n_casts + 1:]):
        dst_ref[...] = src_ref[...].astype(dst_ref.dtype)


def _matmul(a, w, l, out_dtype, tm_pref, tn_pref, name, casts=()):
    M, K = a.shape
    N = w.shape[2]
    tm = _tile(M, tm_pref, BF16_SUBLANES)
    tn = _tile(N, tn_pref, LANES)
    gm, gn = M // tm, N // tn
    in_specs = [pl.BlockSpec((tm, K), lambda m, n: (m, 0)),
                pl.BlockSpec((None, K, tn), lambda m, n: (l, 0, n))]
    out_specs = [pl.BlockSpec((tm, tn), lambda m, n: (m, n))]
    out_shape = [jax.ShapeDtypeStruct((M, N), out_dtype)]
    args = [a, w]
    for src, l_src in casts:
        _, R, C = src.shape
        rb = next(r for r in range(BF16_SUBLANES, R + 1, BF16_SUBLANES)
                  if R % r == 0 and R // r <= gm * gn)
        last = R // rb - 1
        block = functools.partial(lambda m, n, last: jnp.minimum(m * gn + n, last), last=last)
        in_specs.append(pl.BlockSpec((None, rb, C), functools.partial(
            lambda m, n, block, l_src: (l_src, block(m, n), 0), block=block, l_src=l_src)))
        out_specs.append(pl.BlockSpec((rb, C), functools.partial(
            lambda m, n, block: (block(m, n), 0), block=block)))
        out_shape.append(jax.ShapeDtypeStruct((R, C), jnp.bfloat16))
        args.append(src)
    outs = pl.pallas_call(
        _mm_kernel,
        grid=(gm, gn),
        in_specs=in_specs,
        out_specs=out_specs,
        out_shape=out_shape,
        compiler_params=_params("arbitrary", "arbitrary"),
        name=name,
    )(*args)
    if not casts:
        return outs[0]
    return outs[0], [o[None] for o in outs[1:]]


def _swa_kernel(sink_ref, biasp_ref, biasc_ref, q_ref, kp_ref, kc_ref, vp_ref, vc_ref, o_ref):
    i = pl.program_id(1)
    W = WINDOW
    Dh = HEAD_DIM_A
    G = N_Q_A // N_KV_A
    n_pairs = N_Q_A // 2
    dn = (((1,), (1,)), ((), ()))
    zeros = jnp.zeros((W, Dh), jnp.bfloat16)

    def blockdiag(x):
        return jnp.concatenate([jnp.concatenate([x, zeros], axis=1),
                                jnp.concatenate([zeros, x], axis=1)], axis=0)

    kbd_p, kbd_c, vbd_p, vbd_c = [], [], [], []
    for hk in range(N_KV_A):
        sl = slice(hk * Dh, (hk + 1) * Dh)
        kbd_p.append(blockdiag(kp_ref[:, sl]))
        kbd_c.append(blockdiag(kc_ref[:, sl]))
        vbd_p.append(blockdiag(vp_ref[:, sl]))
        vbd_c.append(blockdiag(vc_ref[:, sl]))

    no_prev = jnp.where(i > 0, 0.0, NEG_INF)
    scores = []
    for p in range(n_pairs):
        hk = (2 * p) // G
        q = q_ref[:, p * 2 * Dh:(p + 1) * 2 * Dh]
        scores.append((lax.dot_general(q, kbd_p[hk], dn, preferred_element_type=jnp.float32),
                       lax.dot_general(q, kbd_c[hk], dn, preferred_element_type=jnp.float32)))

    lane = lax.broadcasted_iota(jnp.int32, (W, 2 * Dh), 1)
    for p in range(n_pairs):
        hk = (2 * p) // G
        tp = scores[p][0] * (Dh ** -0.5) + (biasp_ref[p] + no_prev)
        tc = scores[p][1] * (Dh ** -0.5) + biasc_ref[p]
        eps, ecs, invs = [], [], []
        for half in range(2):
            sink = sink_ref[2 * p + half]
            hl = slice(half * W, (half + 1) * W)
            m = jnp.maximum(jnp.max(jnp.maximum(tp[:, hl], tc[:, hl]), axis=-1, keepdims=True), sink)
            ep = jnp.exp(tp[:, hl] - m)
            ec = jnp.exp(tc[:, hl] - m)
            denom = jnp.sum(ep + ec, axis=-1, keepdims=True) + jnp.exp(sink - m)
            eps.append(ep.astype(jnp.bfloat16))
            ecs.append(ec.astype(jnp.bfloat16))
            invs.append(1.0 / denom)
        o = (jnp.dot(jnp.concatenate(eps, axis=1), vbd_p[hk], preferred_element_type=jnp.float32)
             + jnp.dot(jnp.concatenate(ecs, axis=1), vbd_c[hk], preferred_element_type=jnp.float32))
        o = o * jnp.where(lane < Dh, invs[0], invs[1])
        o_ref[:, p * 2 * Dh:(p + 1) * 2 * Dh] = o.astype(o_ref.dtype)


def _swa_bias():
    W = WINDOW
    r = jnp.arange(W)[:, None]
    c = jnp.arange(W)[None, :]
    slopes = jnp.exp2(-8.0 * jnp.arange(1, N_Q_A + 1, dtype=jnp.float32) / N_Q_A)[:, None, None]
    bp = jnp.where(c > r, -slopes * (r + W - c).astype(jnp.float32), NEG_INF)
    bc = jnp.where(c <= r, -slopes * (r - c).astype(jnp.float32), NEG_INF)
    pair = lambda b: b.reshape(N_Q_A // 2, 2, W, W).transpose(0, 2, 1, 3).reshape(N_Q_A // 2, W, 2 * W)
    return pair(bp), pair(bc)


def _swa(proj, sinks, B, S):
    nb = S // WINDOW
    kb = OFF_KA // KV_A
    vb = OFF_VA // KV_A
    cur = lambda col: (lambda b, i: (b * nb + i, col))
    prev = lambda col: (lambda b, i: (b * nb + jnp.maximum(i - 1, 0), col))
    biasp, biasc = _swa_bias()
    bias_spec = pl.BlockSpec(biasp.shape, lambda b, i: (0, 0, 0))
    return pl.pallas_call(
        _swa_kernel,
        grid=(B, nb),
        in_specs=[pl.BlockSpec(memory_space=pltpu.SMEM), bias_spec, bias_spec,
                  pl.BlockSpec((WINDOW, Q_A), cur(OFF_QA // Q_A)),
                  pl.BlockSpec((WINDOW, KV_A), prev(kb)),
                  pl.BlockSpec((WINDOW, KV_A), cur(kb)),
                  pl.BlockSpec((WINDOW, KV_A), prev(vb)),
                  pl.BlockSpec((WINDOW, KV_A), cur(vb))],
        out_specs=pl.BlockSpec((WINDOW, Q_A), lambda b, i: (b * nb + i, 0)),
        out_shape=jax.ShapeDtypeStruct((B * S, Q_A), jnp.bfloat16),
        compiler_params=_params("parallel", "parallel"),
        name="swa",
    )(sinks, biasp, biasc, proj, proj, proj, proj, proj)


def _ssm_kernel(*refs, B, Lc):
    n_u = SSM_WIDTH // COL_BLOCK
    u_refs = refs[:n_u]
    wbr_ref, wbi_ref, a1_ref, a2_ref, cr_ref, ci_ref, d_ref, wg_ref, o_ref, s_ref, x_ref = refs[n_u:]

    @pl.when(pl.program_id(0) == 0)
    def _():
        x_ref[...] = jnp.zeros_like(x_ref)

    R = B * Lc
    u = jnp.concatenate([r[...] for r in u_refs], axis=-1).reshape(R, SSM_WIDTH)
    lpb = SSM_STATES_PER_BLOCK // LANES

    shape = x_ref.shape
    sub = shape[1]

    def time_rows(tile, part, b):
        return pl.ds(((tile % SSM_PACK) * 2 + part) * B + b, Lc, stride=sub)

    for j in range(SSM_BLOCKS):
        uj = u[:, j * LANES:(j + 1) * LANES]
        drive = (jnp.dot(uj, wbr_ref[j], preferred_element_type=jnp.float32),
                 jnp.dot(uj, wbi_ref[j], preferred_element_type=jnp.float32))
        for k in range(lpb):
            tile = j * lpb + k
            for part in range(2):
                for b in range(B):
                    s_ref[tile // SSM_PACK, time_rows(tile, part, b), :] = (
                        drive[part][b * Lc:(b + 1) * Lc, k * LANES:(k + 1) * LANES])

    is_re = lax.broadcasted_iota(jnp.int32, shape, 1) % (2 * B) < B

    def step(t, x):
        rows = pl.ds(pl.multiple_of(t * sub, sub), sub)
        swapped = jnp.where(is_re, pltpu.roll(x, sub - B, 1), pltpu.roll(x, B, 1))
        new = a1_ref[...] * x + a2_ref[...] * swapped + s_ref[:, rows, :]
        s_ref[:, rows, :] = new
        return new

    x_ref[...] = lax.fori_loop(0, Lc, step, x_ref[...], unroll=4)

    def states(j, part):
        return jnp.concatenate(
            [jnp.concatenate([s_ref[t // SSM_PACK, time_rows(t, part, b), :] for b in range(B)], axis=0)
             for t in range(j * lpb, (j + 1) * lpb)], axis=-1).astype(jnp.bfloat16)

    ys = []
    for j in range(SSM_BLOCKS):
        ys.append(jnp.dot(states(j, 0), cr_ref[j], preferred_element_type=jnp.float32)
                  + jnp.dot(states(j, 1), ci_ref[j], preferred_element_type=jnp.float32))
    y = jnp.concatenate(ys, axis=-1) + d_ref[...] * u.astype(jnp.float32)
    y = _gelu_tanh(y).astype(jnp.bfloat16)
    vg = jnp.dot(y, wg_ref[...], preferred_element_type=jnp.float32)
    out = vg[:, :SSM_WIDTH] * _sigmoid(vg[:, SSM_WIDTH:])
    o_ref[...] = out.reshape(B, Lc, SSM_WIDTH).astype(o_ref.dtype)


def _ssm_tables(a_re, a_im, log_dt, b_re, b_im, c_re, c_im, batch):
    f32 = jnp.float32
    lr, li = a_re.astype(f32), a_im.astype(f32)
    dt = jnp.exp(log_dt.astype(f32))[:, None]
    mag = jnp.exp(lr * dt)
    abr, abi = mag * jnp.cos(li * dt), mag * jnp.sin(li * dt)
    den = lr * lr + li * li
    kr = ((abr - 1.0) * lr + abi * li) / den
    ki = (abi * lr - (abr - 1.0) * li) / den
    br, bi = b_re.astype(f32), b_im.astype(f32)
    bbr = kr[..., None] * br - ki[..., None] * bi
    bbi = kr[..., None] * bi + ki[..., None] * br
    gpb, P, N = SSM_GROUPS_PER_BLOCK, SSM_GROUP, SSM_STATE
    eye = jnp.eye(gpb, dtype=f32)

    def in_blocks(w):
        w = w.reshape(SSM_BLOCKS, gpb, N, P)
        return jnp.einsum('jgnp,gh->jgphn', w, eye).reshape(SSM_BLOCKS, gpb * P, gpb * N)

    def out_blocks(w):
        w = w.reshape(SSM_BLOCKS, gpb, P, N)
        return jnp.einsum('jgpn,gh->jgnhp', w, eye).reshape(SSM_BLOCKS, gpb * N, gpb * P)

    wbr = in_blocks(bbr).astype(jnp.bfloat16)
    wbi = in_blocks(bbi).astype(jnp.bfloat16)
    cr = out_blocks(c_re.astype(f32)).astype(jnp.bfloat16)
    ci = out_blocks(-c_im.astype(f32)).astype(jnp.bfloat16)
    pairs = SSM_STATE_TILES // SSM_PACK
    a1 = jnp.broadcast_to(abr.reshape(pairs, SSM_PACK, 1, 1, LANES), (pairs, SSM_PACK, 2, batch, LANES))
    sign = jnp.array([-1.0, 1.0], f32).reshape(1, 1, 2, 1, 1)
    a2 = jnp.broadcast_to(abi.reshape(pairs, SSM_PACK, 1, 1, LANES) * sign, (pairs, SSM_PACK, 2, batch, LANES))
    rows = SSM_PACK * 2 * batch
    return wbr, wbi, a1.reshape(pairs, rows, LANES), a2.reshape(pairs, rows, LANES), cr, ci


def _ssm(proj, tables, d_skip, w_glu, l, B, S):
    wbr, wbi, a1, a2, cr, ci = tables
    n_in = proj.shape[1]
    Lc = _tile(S, 256, BF16_SUBLANES)
    rows = SSM_PACK * 2 * B
    assert rows == 8, "packed scan layout needs (lane tiles per vreg) * 2 * batch == 8 sublanes"
    pairs = SSM_STATE_TILES // SSM_PACK
    full = lambda *shape: pl.BlockSpec(shape, lambda c: (0,) * len(shape))
    n_u = SSM_WIDTH // COL_BLOCK
    u_specs = [pl.BlockSpec((B, Lc, COL_BLOCK), functools.partial(lambda c, s: (0, c, OFF_US // COL_BLOCK + s), s=s))
               for s in range(n_u)]
    proj3 = proj.reshape(B, S, n_in)
    return pl.pallas_call(
        functools.partial(_ssm_kernel, B=B, Lc=Lc),
        grid=(S // Lc,),
        in_specs=u_specs + [full(*wbr.shape), full(*wbi.shape), full(*a1.shape), full(*a2.shape),
                            full(*cr.shape), full(*ci.shape), full(1, SSM_WIDTH),
                            pl.BlockSpec((None,) + w_glu.shape[1:], lambda c: (l, 0, 0))],
        out_specs=pl.BlockSpec((B, Lc, SSM_WIDTH), lambda c: (0, c, 0)),
        out_shape=jax.ShapeDtypeStruct((B, S, SSM_WIDTH), jnp.bfloat16),
        scratch_shapes=[pltpu.VMEM((pairs, rows * Lc, LANES), jnp.float32),
                        pltpu.VMEM((pairs, rows, LANES), jnp.float32)],
        compiler_params=_params("arbitrary"),
        name="s5_ssm",
    )(*([proj3] * n_u), wbr, wbi, a1, a2, cr, ci, d_skip.reshape(1, SSM_WIDTH), w_glu)


def _sb_kernel(*refs, tq, heads):
    q_refs, k_refs, v_refs = refs[:heads], refs[heads:2 * heads], refs[2 * heads:3 * heads]
    o_ref = refs[3 * heads]
    qi = pl.program_id(2)
    hd = HEAD_DIM_SB
    row = lax.broadcasted_iota(jnp.int32, (tq, tq), 0)
    col = lax.broadcasted_iota(jnp.int32, (tq, tq), 1)
    after = jnp.where(row > col, 1.0, 0.0).astype(jnp.bfloat16)
    after2 = jnp.concatenate([after, after], axis=0)
    causal = col < row
    dn = (((1,), (1,)), ((), ()))
    scale = hd ** -0.5
    log2e = math.log2(math.e)
    ln2 = math.log(2.0)

    def blocks(j, state, diagonal):
        start = pl.multiple_of(j * tq, tq)
        zs = []
        for h in range(heads):
            k = k_refs[h][pl.ds(start, tq), :]
            zs.append(lax.dot_general(qs[h], k, dn, preferred_element_type=jnp.float32) * scale)
        betweens, log_betas, sums = [], [], []
        for h in range(heads):
            z = zs[h]
            sp = jnp.maximum(z, 0.0) + jnp.log(1.0 + jnp.exp2(jnp.abs(z) * (-log2e)))
            log_betas.append(z - sp)
            if diagonal:
                sp = jnp.where(causal, sp, 0.0)
            hi = sp.astype(jnp.bfloat16)
            lo = (sp - hi.astype(jnp.float32)).astype(jnp.bfloat16)
            betweens.append(jnp.dot(jnp.concatenate([hi, lo], axis=1), after2,
                                    preferred_element_type=jnp.float32))
            sums.append(jnp.sum(sp, axis=-1, keepdims=True))
        out = []
        for h in range(heads):
            carry, acc = state[2 * h], state[2 * h + 1]
            w = jnp.exp2((log_betas[h] - betweens[h] - carry) * log2e)
            if diagonal:
                w = jnp.where(causal, w, 0.0)
            v = v_refs[h][pl.ds(start, tq), :]
            acc = acc + jnp.dot(w.astype(jnp.bfloat16), v, preferred_element_type=jnp.float32)
            out.extend((carry + sums[h], acc))
        return tuple(out)

    qs = [q_refs[h][...] for h in range(heads)]
    state = blocks(qi, (jnp.zeros((tq, 1), jnp.float32), jnp.zeros((tq, hd), jnp.float32)) * heads, True)

    def min_carry(state):
        m = state[0]
        for h in range(1, heads):
            m = jnp.minimum(m, state[2 * h])
        return jnp.min(m)

    def cond(loop):
        return jnp.logical_and(loop[0] < qi, loop[1] < SB_EXP_UNDERFLOW)

    def body(loop):
        state = blocks(qi - 1 - loop[0], loop[2:], False)
        return (loop[0] + 1, min_carry(state)) + state

    state = lax.while_loop(cond, body, (jnp.int32(0), min_carry(state)) + state)[2:]
    for h in range(heads):
        o_ref[:, h * hd:(h + 1) * hd] = state[2 * h + 1].astype(o_ref.dtype)


def _stick_breaking(proj, B, S):
    tq = _tile(S, 256, LANES)
    nq = S // tq
    heads = 4
    hd = HEAD_DIM_SB

    def head_specs(rows, off, row_index):
        return [pl.BlockSpec((rows, hd), functools.partial(
                    lambda b, g, i, s: (row_index(b, i), off // hd + g * heads + s), s=s))
                for s in range(heads)]

    return pl.pallas_call(
        functools.partial(_sb_kernel, tq=tq, heads=heads),
        grid=(B, N_SB // heads, nq),
        in_specs=(head_specs(tq, OFF_QS, lambda b, i: b * nq + i)
                  + head_specs(S, OFF_KS, lambda b, i: b)
                  + head_specs(S, OFF_VS, lambda b, i: b)),
        out_specs=pl.BlockSpec((tq, heads * hd), lambda b, g, i: (b * nq + i, g)),
        out_shape=jax.ShapeDtypeStruct((B * S, SB_W), jnp.bfloat16),
        compiler_params=_params("parallel", "parallel", "arbitrary"),
        name="stick_breaking",
    )(*([proj] * (3 * heads)))


def _merge_kernel(ya_ref, ys_ref, yc_ref, w_ref, *rest):
    o_ref = rest[-1]
    g_refs = rest[:-1]
    subs = len(g_refs) // N_BRANCH
    acc = None
    for k, y_ref in enumerate((ya_ref, ys_ref, yc_ref)):
        t = jnp.dot(y_ref[...], w_ref[k], preferred_element_type=jnp.float32)
        g = jnp.concatenate([g_refs[k * subs + s][...] for s in range(subs)], axis=-1)
        t = _sigmoid(g.astype(jnp.float32)) * t
        acc = t if acc is None else acc + t
    o_ref[...] = acc.astype(o_ref.dtype)


def _merge(y_a, y_s, y_c, w_branch, l, proj, D):
    M = y_a.shape[0]
    tm = _tile(M, 1024, BF16_SUBLANES)
    tn = _tile(D, 512, COL_BLOCK)
    subs = tn // COL_BLOCK
    gate_specs = [pl.BlockSpec((tm, COL_BLOCK), functools.partial(
                      lambda m, n, k, s: (m, (OFF_G + k * D) // COL_BLOCK + n * subs + s), k=k, s=s))
                  for k in range(N_BRANCH) for s in range(subs)]
    row = pl.BlockSpec((tm, BRANCH_WIDTH), lambda m, n: (m, 0))
    return pl.pallas_call(
        _merge_kernel,
        grid=(M // tm, D // tn),
        in_specs=[row, row, row,
                  pl.BlockSpec((None, N_BRANCH, BRANCH_WIDTH, tn), lambda m, n: (l, 0, 0, n))] + gate_specs,
        out_specs=pl.BlockSpec((tm, tn), lambda m, n: (m, n)),
        out_shape=jax.ShapeDtypeStruct((M, D), jnp.bfloat16),
        compiler_params=_params("parallel", "arbitrary"),
        name="gated_merge",
    )(y_a, y_s, y_c, w_branch, *([proj] * len(gate_specs)))


def _ffn_up_kernel(h_ref, halo_ref, wa_ref, wb_ref, cwa_ref, cwb_ref, cba_ref, cbb_ref,
                   o_ref, a_scr, *, tm, tiles_per_seq):
    H = BF16_SUBLANES

    @pl.when(pl.program_id(1) == 0)
    def _():
        first = pl.program_id(0) % tiles_per_seq == 0
        halo = halo_ref[...]
        a_scr[0:H, :] = jnp.where(first, jnp.zeros_like(halo), halo)
        a_scr[H:, :] = h_ref[...]

    a = a_scr[...]

    def conv(w_ref, cw_ref, cb_ref):
        u = jnp.dot(a, w_ref[...], preferred_element_type=jnp.float32)
        u1 = pltpu.roll(u, 1, 0)
        u2 = pltpu.roll(u, 2, 0)
        return (cb_ref[...] + cw_ref[0:1, :] * u2[H:] + cw_ref[1:2, :] * u1[H:]
                + cw_ref[2:3, :] * u[H:])

    ga = conv(wa_ref, cwa_ref, cba_ref)
    gb = conv(wb_ref, cwb_ref, cbb_ref)
    o_ref[...] = (_gelu_tanh(ga) * gb).astype(o_ref.dtype)


def _ffn_up(h, w_up, l, conv_w, conv_b, S):
    M, D = h.shape
    F = w_up.shape[2] // 2
    H = BF16_SUBLANES
    tm = _tile(S, 1024, H)
    tn = _tile(F, 512, LANES)
    nf = F // tn
    return pl.pallas_call(
        functools.partial(_ffn_up_kernel, tm=tm, tiles_per_seq=S // tm),
        grid=(M // tm, nf),
        in_specs=[pl.BlockSpec((tm, D), lambda m, n: (m, 0)),
                  pl.BlockSpec((H, D), lambda m, n: (jnp.maximum(m * (tm // H) - 1, 0), 0)),
                  pl.BlockSpec((None, D, tn), lambda m, n: (l, 0, n)),
                  pl.BlockSpec((None, D, tn), lambda m, n: (l, 0, nf + n)),
                  pl.BlockSpec((CONV_W, tn), lambda m, n: (0, n)),
                  pl.BlockSpec((CONV_W, tn), lambda m, n: (0, nf + n)),
                  pl.BlockSpec((1, tn), lambda m, n: (0, n)),
                  pl.BlockSpec((1, tn), lambda m, n: (0, nf + n))],
        out_specs=pl.BlockSpec((tm, tn), lambda m, n: (m, n)),
        out_shape=jax.ShapeDtypeStruct((M, F), jnp.bfloat16),
        scratch_shapes=[pltpu.VMEM((H + tm, D), jnp.bfloat16)],
        compiler_params=_params("parallel", "arbitrary"),
        name="ffn_up_conv_gate",
    )(h, h, w_up, w_up, conv_w, conv_w, conv_b.reshape(1, 2 * F), conv_b.reshape(1, 2 * F))


def kernel(x, c, w_ada, b_ada, ada_table, norm_mix_pre, norm_mix_post, norm_ffn_pre, norm_ffn_post,
           w_in, attn_sinks, ssm_a_re, ssm_a_im, ssm_log_dt, ssm_b_re, ssm_b_im, ssm_c_re, ssm_c_im,
           ssm_d, ssm_w_glu, w_branch, w_out, ffn_w_up, ffn_conv_w, ffn_conv_b, ffn_w_down):
    B, S, D = x.shape
    depth = w_in.shape[0]
    T = B * S
    bf16 = jnp.bfloat16

    w_in_l = w_in[:1].astype(bf16)
    w_glu_b = ssm_w_glu.astype(bf16)
    w_branch_rows = w_branch.reshape(depth, N_BRANCH * BRANCH_WIDTH, D)

    mod_shared = _ada_proj(c, w_ada, b_ada)
    h = _normmod(x, norm_mix_pre[0], mod_shared, ada_table[0], 0)
    for l in range(depth):
        tab = ada_table[l]
        casts = [(ffn_w_up, l), (ffn_w_down, l), (w_out, l), (w_branch_rows, l)]
        if l + 1 < depth:
            casts.append((w_in, l + 1))
        proj, cast = _matmul(h.reshape(T, D), w_in_l, 0, bf16, 1024, 768, "in_proj", casts=casts)
        w_up_l, w_down_l, w_out_l, w_branch_l = cast[:4]
        w_branch_l = w_branch_l.reshape(1, N_BRANCH, BRANCH_WIDTH, D)
        if l + 1 < depth:
            w_in_l = cast[4]
        y_a = _swa(proj, attn_sinks[l], B, S)
        tables = _ssm_tables(ssm_a_re[l], ssm_a_im[l], ssm_log_dt[l], ssm_b_re[l], ssm_b_im[l],
                             ssm_c_re[l], ssm_c_im[l], B)
        y_s = _ssm(proj, tables, ssm_d[l], w_glu_b, l, B, S).reshape(T, SSM_WIDTH)
        y_c = _stick_breaking(proj, B, S)
        merged = _merge(y_a, y_s, y_c, w_branch_l, 0, proj, D)
        y = _matmul(merged, w_out_l, 0, bf16, 1024, 1024, "out_proj")
        x, h = _postnorm_residual(x, y, norm_mix_post[l], mod_shared, tab, 2,
                                  next_norm=(norm_ffn_pre[l], tab, 3))

        gact = _ffn_up(h.reshape(T, D), w_up_l, 0, ffn_conv_w[l], ffn_conv_b[l], S)
        y = _matmul(gact, w_down_l, 0, bf16, 1024, 512, "ffn_down")
        if l + 1 < depth:
            x, h = _postnorm_residual(x, y, norm_ffn_post[l], mod_shared, tab, 5,
                                      next_norm=(norm_mix_pre[l + 1], ada_table[l + 1], 0))
        else:
            x = _postnorm_residual(x, y, norm_ffn_post[l], mod_shared, tab, 5)
    return x
```

```python
import functools
import math

import jax
import jax.numpy as jnp
from jax import lax
from jax.experimental import pallas as pl
from jax.experimental.pallas import tpu as pltpu

HEAD_DIM_A = 64
N_Q_A = 16
N_KV_A = 2
WINDOW = 128
SSM_WIDTH = 1024
SSM_GROUP = 16
SSM_GROUPS = SSM_WIDTH // SSM_GROUP
SSM_STATE = 64
N_SB = 8
HEAD_DIM_SB = 128
BRANCH_WIDTH = 1024
N_BRANCH = 3
CONV_W = 3
N_MOD = 6
EPS = 1e-6
NEG_INF = -1e30

SB_EXP_UNDERFLOW = 105.0

Q_A = N_Q_A * HEAD_DIM_A
KV_A = N_KV_A * HEAD_DIM_A
SB_W = N_SB * HEAD_DIM_SB

OFF_QA = 0
OFF_KA = OFF_QA + Q_A
OFF_VA = OFF_KA + KV_A
OFF_US = OFF_VA + KV_A
OFF_QS = OFF_US + SSM_WIDTH
OFF_KS = OFF_QS + SB_W
OFF_VS = OFF_KS + SB_W
OFF_G = OFF_VS + SB_W
COL_BLOCK = 256

LANES = 128
BF16_SUBLANES = 16
VMEM_LIMIT_BYTES = 56 * 1024 * 1024

SSM_GROUPS_PER_BLOCK = LANES // SSM_GROUP
SSM_BLOCKS = SSM_GROUPS // SSM_GROUPS_PER_BLOCK
SSM_STATES_PER_BLOCK = SSM_GROUPS_PER_BLOCK * SSM_STATE
SSM_STATES = SSM_GROUPS * SSM_STATE
SSM_STATE_TILES = SSM_STATES // LANES
SSM_PACK = 2


def _tile(dim, pref, align):
    t = (min(pref, dim) // align) * align
    while t >= align:
        if dim % t == 0:
            return t
        t -= align
    return dim


def _params(*sem):
    return pltpu.CompilerParams(dimension_semantics=sem, vmem_limit_bytes=VMEM_LIMIT_BYTES)


def _gelu_tanh(x):
    return 0.5 * x * (1.0 + jnp.tanh(math.sqrt(2.0 / math.pi) * (x + 0.044715 * (x * x * x))))


def _sigmoid(x):
    return 1.0 / (1.0 + jnp.exp(-x))


def _ada_kernel(c_ref, w_ref, b_ref, o_ref):
    c = c_ref[...]
    a = (c * _sigmoid(c)).astype(jnp.bfloat16)
    o_ref[...] = jnp.dot(a, w_ref[...].astype(jnp.bfloat16),
                         preferred_element_type=jnp.float32) + b_ref[...]


def _ada_proj(c, w_ada, b_ada):
    B, D = c.shape
    N = w_ada.shape[1]
    rows = 8
    c_pad = jnp.zeros((rows, D), jnp.float32).at[:B].set(c)
    tn = _tile(N, 512, LANES)
    out = pl.pallas_call(
        _ada_kernel,
        grid=(N // tn,),
        in_specs=[pl.BlockSpec((rows, D), lambda n: (0, 0)),
                  pl.BlockSpec((D, tn), lambda n: (0, n)),
                  pl.BlockSpec((1, tn), lambda n: (0, n))],
        out_specs=pl.BlockSpec((rows, tn), lambda n: (0, n)),
        out_shape=jax.ShapeDtypeStruct((rows, N), jnp.float32),
        compiler_params=_params("parallel"),
        name="ada_proj",
    )(c_pad, w_ada, b_ada.reshape(1, N))
    return out[:B].reshape(B, N_MOD, D)


def _normmod_kernel(x_ref, g_ref, mod_ref, tab_ref, h_ref, *, shift_idx):
    x = x_ref[0]
    ms = jnp.mean(x * x, axis=-1, keepdims=True)
    y = x * lax.rsqrt(ms + EPS) * g_ref[...]
    shift = mod_ref[0, shift_idx:shift_idx + 1, :] + tab_ref[shift_idx:shift_idx + 1, :]
    scale = mod_ref[0, shift_idx + 1:shift_idx + 2, :] + tab_ref[shift_idx + 1:shift_idx + 2, :]
    h_ref[0] = (y * (1.0 + scale) + shift).astype(h_ref.dtype)


def _normmod(x, g, mod_shared, tab, shift_idx):
    B, S, D = x.shape
    tr = _tile(S, 256, BF16_SUBLANES)
    return pl.pallas_call(
        functools.partial(_normmod_kernel, shift_idx=shift_idx),
        grid=(B, S // tr),
        in_specs=[pl.BlockSpec((1, tr, D), lambda b, i: (b, i, 0)),
                  pl.BlockSpec((1, D), lambda b, i: (0, 0)),
                  pl.BlockSpec((1, N_MOD, D), lambda b, i: (b, 0, 0)),
                  pl.BlockSpec((N_MOD, D), lambda b, i: (0, 0))],
        out_specs=pl.BlockSpec((1, tr, D), lambda b, i: (b, i, 0)),
        out_shape=jax.ShapeDtypeStruct((B, S, D), jnp.bfloat16),
        compiler_params=_params("parallel", "parallel"),
        name="normmod",
    )(x, g.reshape(1, D), mod_shared, tab)


def _postnorm_kernel(x_ref, y_ref, g_ref, mod_ref, tab_ref, *rest, gate_idx, next_shift_idx):
    y = y_ref[0].astype(jnp.float32)
    ms = jnp.mean(y * y, axis=-1, keepdims=True)
    yn = y * lax.rsqrt(ms + EPS) * g_ref[...]
    gate = mod_ref[0, gate_idx:gate_idx + 1, :] + tab_ref[gate_idx:gate_idx + 1, :]
    x = x_ref[0] + gate * yn
    if next_shift_idx is None:
        (o_ref,) = rest
        o_ref[0] = x
        return
    gn_ref, tabn_ref, o_ref, h_ref = rest
    o_ref[0] = x
    ms = jnp.mean(x * x, axis=-1, keepdims=True)
    xn = x * lax.rsqrt(ms + EPS) * gn_ref[...]
    k = next_shift_idx
    shift = mod_ref[0, k:k + 1, :] + tabn_ref[k:k + 1, :]
    scale = mod_ref[0, k + 1:k + 2, :] + tabn_ref[k + 1:k + 2, :]
    h_ref[0] = (xn * (1.0 + scale) + shift).astype(h_ref.dtype)


def _postnorm_residual(x, y, g, mod_shared, tab, gate_idx, next_norm=None):
    B, S, D = x.shape
    tr = _tile(S, 256, BF16_SUBLANES)
    row = pl.BlockSpec((1, tr, D), lambda b, i: (b, i, 0))
    vec = pl.BlockSpec((1, D), lambda b, i: (0, 0))
    tabs = pl.BlockSpec((N_MOD, D), lambda b, i: (0, 0))
    in_specs = [row, row, vec, pl.BlockSpec((1, N_MOD, D), lambda b, i: (b, 0, 0)), tabs]
    args = [x, y.reshape(B, S, D), g.reshape(1, D), mod_shared, tab]
    x_shape = jax.ShapeDtypeStruct((B, S, D), jnp.float32)
    if next_norm is None:
        out_specs, out_shape, next_shift_idx = row, x_shape, None
    else:
        g_next, tab_next, next_shift_idx = next_norm
        in_specs += [vec, tabs]
        args += [g_next.reshape(1, D), tab_next]
        out_specs = (row, row)
        out_shape = (x_shape, jax.ShapeDtypeStruct((B, S, D), jnp.bfloat16))
    return pl.pallas_call(
        functools.partial(_postnorm_kernel, gate_idx=gate_idx, next_shift_idx=next_shift_idx),
        grid=(B, S // tr),
        in_specs=in_specs,
        out_specs=out_specs,
        out_shape=out_shape,
        compiler_params=_params("parallel", "parallel"),
        name="postnorm_residual",
    )(*args)


def _mm_kernel(a_ref, w_ref, *rest):
    n_casts = len(rest) // 2
    o_ref = rest[n_casts]
    o_ref[...] = jnp.dot(a_ref[...], w_ref[...],
                         preferred_element_type=jnp.float32).astype(o_ref.dtype)
    for src_ref, dst_ref in zip(rest[:n_casts], rest[n_casts + 1:]):
        dst_ref[...] = src_ref[...].astype(dst_ref.dtype)


def _matmul(a, w, l, out_dtype, tm_pref, tn_pref, name, casts=()):
    M, K = a.shape
    N = w.shape[2]
    tm = _tile(M, tm_pref, BF16_SUBLANES)
    tn = _tile(N, tn_pref, LANES)
    gm, gn = M // tm, N // tn
    in_specs = [pl.BlockSpec((tm, K), lambda m, n: (m, 0)),
                pl.BlockSpec((None, K, tn), lambda m, n: (l, 0, n))]
    out_specs = [pl.BlockSpec((tm, tn), lambda m, n: (m, n))]
    out_shape = [jax.ShapeDtypeStruct((M, N), out_dtype)]
    args = [a, w]
    for src, l_src in casts:
        _, R, C = src.shape
        rb = next(r for r in range(BF16_SUBLANES, R + 1, BF16_SUBLANES)
                  if R % r == 0 and R // r <= gm * gn)
        last = R // rb - 1
        block = functools.partial(lambda m, n, last: jnp.minimum(m * gn + n, last), last=last)
        in_specs.append(pl.BlockSpec((None, rb, C), functools.partial(
            lambda m, n, block, l_src: (l_src, block(m, n), 0), block=block, l_src=l_src)))
        out_specs.append(pl.BlockSpec((rb, C), functools.partial(
            lambda m, n, block: (block(m, n), 0), block=block)))
        out_shape.append(jax.ShapeDtypeStruct((R, C), jnp.bfloat16))
        args.append(src)
    outs = pl.pallas_call(
        _mm_kernel,
        grid=(gm, gn),
        in_specs=in_specs,
        out_specs=out_specs,
        out_shape=out_shape,
        compiler_params=_params("arbitrary", "arbitrary"),
        name=name,
    )(*args)
    if not casts:
        return outs[0]
    return outs[0], [o[None] for o in outs[1:]]


def _swa_kernel(sink_ref, biasp_ref, biasc_ref, q_ref, kp_ref, kc_ref, vp_ref, vc_ref, o_ref):
    i = pl.program_id(1)
    W = WINDOW
    Dh = HEAD_DIM_A
    G = N_Q_A // N_KV_A
    n_pairs = N_Q_A // 2
    dn = (((1,), (1,)), ((), ()))
    zeros = jnp.zeros((W, Dh), jnp.bfloat16)

    def blockdiag(x):
        return jnp.concatenate([jnp.concatenate([x, zeros], axis=1),
                                jnp.concatenate([zeros, x], axis=1)], axis=0)

    kbd_p, kbd_c, vbd_p, vbd_c = [], [], [], []
    for hk in range(N_KV_A):
        sl = slice(hk * Dh, (hk + 1) * Dh)
        kbd_p.append(blockdiag(kp_ref[:, sl]))
        kbd_c.append(blockdiag(kc_ref[:, sl]))
        vbd_p.append(blockdiag(vp_ref[:, sl]))
        vbd_c.append(blockdiag(vc_ref[:, sl]))

    no_prev = jnp.where(i > 0, 0.0, NEG_INF)
    scores = []
    for p in range(n_pairs):
        hk = (2 * p) // G
        q = q_ref[:, p * 2 * Dh:(p + 1) * 2 * Dh]
        scores.append((lax.dot_general(q, kbd_p[hk], dn, preferred_element_type=jnp.float32),
                       lax.dot_general(q, kbd_c[hk], dn, preferred_element_type=jnp.float32)))

    lane = lax.broadcasted_iota(jnp.int32, (W, 2 * Dh), 1)
    for p in range(n_pairs):
        hk = (2 * p) // G
        tp = scores[p][0] * (Dh ** -0.5) + (biasp_ref[p] + no_prev)
        tc = scores[p][1] * (Dh ** -0.5) + biasc_ref[p]
        eps, ecs, invs = [], [], []
        for half in range(2):
            sink = sink_ref[2 * p + half]
            hl = slice(half * W, (half + 1) * W)
            m = jnp.maximum(jnp.max(jnp.maximum(tp[:, hl], tc[:, hl]), axis=-1, keepdims=True), sink)
            ep = jnp.exp(tp[:, hl] - m)
            ec = jnp.exp(tc[:, hl] - m)
            denom = jnp.sum(ep + ec, axis=-1, keepdims=True) + jnp.exp(sink - m)
            eps.append(ep.astype(jnp.bfloat16))
            ecs.append(ec.astype(jnp.bfloat16))
            invs.append(1.0 / denom)
        o = (jnp.dot(jnp.concatenate(eps, axis=1), vbd_p[hk], preferred_element_type=jnp.float32)
             + jnp.dot(jnp.concatenate(ecs, axis=1), vbd_c[hk], preferred_element_type=jnp.float32))
        o = o * jnp.where(lane < Dh, invs[0], invs[1])
        o_ref[:, p * 2 * Dh:(p + 1) * 2 * Dh] = o.astype(o_ref.dtype)


def _swa_bias():
    W = WINDOW
    r = jnp.arange(W)[:, None]
    c = jnp.arange(W)[None, :]
    slopes = jnp.exp2(-8.0 * jnp.arange(1, N_Q_A + 1, dtype=jnp.float32) / N_Q_A)[:, None, None]
    bp = jnp.where(c > r, -slopes * (r + W - c).astype(jnp.float32), NEG_INF)
    bc = jnp.where(c <= r, -slopes * (r - c).astype(jnp.float32), NEG_INF)
    pair = lambda b: b.reshape(N_Q_A // 2, 2, W, W).transpose(0, 2, 1, 3).reshape(N_Q_A // 2, W, 2 * W)
    return pair(bp), pair(bc)


def _swa(proj, sinks, B, S):
    nb = S // WINDOW
    kb = OFF_KA // KV_A
    vb = OFF_VA // KV_A
    cur = lambda col: (lambda b, i: (b * nb + i, col))
    prev = lambda col: (lambda b, i: (b * nb + jnp.maximum(i - 1, 0), col))
    biasp, biasc = _swa_bias()
    bias_spec = pl.BlockSpec(biasp.shape, lambda b, i: (0, 0, 0))
    return pl.pallas_call(
        _swa_kernel,
        grid=(B, nb),
        in_specs=[pl.BlockSpec(memory_space=pltpu.SMEM), bias_spec, bias_spec,
                  pl.BlockSpec((WINDOW, Q_A), cur(OFF_QA // Q_A)),
                  pl.BlockSpec((WINDOW, KV_A), prev(kb)),
                  pl.BlockSpec((WINDOW, KV_A), cur(kb)),
                  pl.BlockSpec((WINDOW, KV_A), prev(vb)),
                  pl.BlockSpec((WINDOW, KV_A), cur(vb))],
        out_specs=pl.BlockSpec((WINDOW, Q_A), lambda b, i: (b * nb + i, 0)),
        out_shape=jax.ShapeDtypeStruct((B * S, Q_A), jnp.bfloat16),
        compiler_params=_params("parallel", "parallel"),
        name="swa",
    )(sinks, biasp, biasc, proj, proj, proj, proj, proj)


def _ssm_kernel(*refs, B, Lc):
    n_u = SSM_WIDTH // COL_BLOCK
    u_refs = refs[:n_u]
    wbr_ref, wbi_ref, a1_ref, a2_ref, cr_ref, ci_ref, d_ref, wg_ref, o_ref, s_ref, x_ref = refs[n_u:]

    @pl.when(pl.program_id(0) == 0)
    def _():
        x_ref[...] = jnp.zeros_like(x_ref)

    R = B * Lc
    u = jnp.concatenate([r[...] for r in u_refs], axis=-1).reshape(R, SSM_WIDTH)
    lpb = SSM_STATES_PER_BLOCK // LANES

    shape = x_ref.shape
    sub = shape[1]

    def time_rows(tile, part, b):
        return pl.ds(((tile % SSM_PACK) * 2 + part) * B + b, Lc, stride=sub)

    for j in range(SSM_BLOCKS):
        uj = u[:, j * LANES:(j + 1) * LANES]
        drive = (jnp.dot(uj, wbr_ref[j], preferred_element_type=jnp.float32),
                 jnp.dot(uj, wbi_ref[j], preferred_element_type=jnp.float32))
        for k in range(lpb):
            tile = j * lpb + k
            for part in range(2):
                for b in range(B):
                    s_ref[tile // SSM_PACK, time_rows(tile, part, b), :] = (
                        drive[part][b * Lc:(b + 1) * Lc, k * LANES:(k + 1) * LANES])

    is_re = lax.broadcasted_iota(jnp.int32, shape, 1) % (2 * B) < B

    def step(t, x):
        rows = pl.ds(pl.multiple_of(t * sub, sub), sub)
        swapped = jnp.where(is_re, pltpu.roll(x, sub - B, 1), pltpu.roll(x, B, 1))
        new = a1_ref[...] * x + a2_ref[...] * swapped + s_ref[:, rows, :]
        s_ref[:, rows, :] = new
        return new

    x_ref[...] = lax.fori_loop(0, Lc, step, x_ref[...], unroll=4)

    def states(j, part):
        return jnp.concatenate(
            [jnp.concatenate([s_ref[t // SSM_PACK, time_rows(t, part, b), :] for b in range(B)], axis=0)
             for t in range(j * lpb, (j + 1) * lpb)], axis=-1).astype(jnp.bfloat16)

    ys = []
    for j in range(SSM_BLOCKS):
        ys.append(jnp.dot(states(j, 0), cr_ref[j], preferred_element_type=jnp.float32)
                  + jnp.dot(states(j, 1), ci_ref[j], preferred_element_type=jnp.float32))
    y = jnp.concatenate(ys, axis=-1) + d_ref[...] * u.astype(jnp.float32)
    y = _gelu_tanh(y).astype(jnp.bfloat16)
    vg = jnp.dot(y, wg_ref[...], preferred_element_type=jnp.float32)
    out = vg[:, :SSM_WIDTH] * _sigmoid(vg[:, SSM_WIDTH:])
    o_ref[...] = out.reshape(B, Lc, SSM_WIDTH).astype(o_ref.dtype)


def _ssm_tables(a_re, a_im, log_dt, b_re, b_im, c_re, c_im, batch):
    f32 = jnp.float32
    lr, li = a_re.astype(f32), a_im.astype(f32)
    dt = jnp.exp(log_dt.astype(f32))[:, None]
    mag = jnp.exp(lr * dt)
    abr, abi = mag * jnp.cos(li * dt), mag * jnp.sin(li * dt)
    den = lr * lr + li * li
    kr = ((abr - 1.0) * lr + abi * li) / den
    ki = (abi * lr - (abr - 1.0) * li) / den
    br, bi = b_re.astype(f32), b_im.astype(f32)
    bbr = kr[..., None] * br - ki[..., None] * bi
    bbi = kr[..., None] * bi + ki[..., None] * br
    gpb, P, N = SSM_GROUPS_PER_BLOCK, SSM_GROUP, SSM_STATE
    eye = jnp.eye(gpb, dtype=f32)

    def in_blocks(w):
        w = w.reshape(SSM_BLOCKS, gpb, N, P)
        return jnp.einsum('jgnp,gh->jgphn', w, eye).reshape(SSM_BLOCKS, gpb * P, gpb * N)

    def out_blocks(w):
        w = w.reshape(SSM_BLOCKS, gpb, P, N)
        return jnp.einsum('jgpn,gh->jgnhp', w, eye).reshape(SSM_BLOCKS, gpb * N, gpb * P)

    wbr = in_blocks(bbr).astype(jnp.bfloat16)
    wbi = in_blocks(bbi).astype(jnp.bfloat16)
    cr = out_blocks(c_re.astype(f32)).astype(jnp.bfloat16)
    ci = out_blocks(-c_im.astype(f32)).astype(jnp.bfloat16)
    pairs = SSM_STATE_TILES // SSM_PACK
    a1 = jnp.broadcast_to(abr.reshape(pairs, SSM_PACK, 1, 1, LANES), (pairs, SSM_PACK, 2, batch, LANES))
    sign = jnp.array([-1.0, 1.0], f32).reshape(1, 1, 2, 1, 1)
    a2 = jnp.broadcast_to(abi.reshape(pairs, SSM_PACK, 1, 1, LANES) * sign, (pairs, SSM_PACK, 2, batch, LANES))
    rows = SSM_PACK * 2 * batch
    return wbr, wbi, a1.reshape(pairs, rows, LANES), a2.reshape(pairs, rows, LANES), cr, ci


def _ssm(proj, tables, d_skip, w_glu, l, B, S):
    wbr, wbi, a1, a2, cr, ci = tables
    n_in = proj.shape[1]
    Lc = _tile(S, 256, BF16_SUBLANES)
    rows = SSM_PACK * 2 * B
    assert rows == 8, "packed scan layout needs (lane tiles per vreg) * 2 * batch == 8 sublanes"
    pairs = SSM_STATE_TILES // SSM_PACK
    full = lambda *shape: pl.BlockSpec(shape, lambda c: (0,) * len(shape))
    n_u = SSM_WIDTH // COL_BLOCK
    u_specs = [pl.BlockSpec((B, Lc, COL_BLOCK), functools.partial(lambda c, s: (0, c, OFF_US // COL_BLOCK + s), s=s))
               for s in range(n_u)]
    proj3 = proj.reshape(B, S, n_in)
    return pl.pallas_call(
        functools.partial(_ssm_kernel, B=B, Lc=Lc),
        grid=(S // Lc,),
        in_specs=u_specs + [full(*wbr.shape), full(*wbi.shape), full(*a1.shape), full(*a2.shape),
                            full(*cr.shape), full(*ci.shape), full(1, SSM_WIDTH),
                            pl.BlockSpec((None,) + w_glu.shape[1:], lambda c: (l, 0, 0))],
        out_specs=pl.BlockSpec((B, Lc, SSM_WIDTH), lambda c: (0, c, 0)),
        out_shape=jax.ShapeDtypeStruct((B, S, SSM_WIDTH), jnp.bfloat16),
        scratch_shapes=[pltpu.VMEM((pairs, rows * Lc, LANES), jnp.float32),
                        pltpu.VMEM((pairs, rows, LANES), jnp.float32)],
        compiler_params=_params("arbitrary"),
        name="s5_ssm",
    )(*([proj3] * n_u), wbr, wbi, a1, a2, cr, ci, d_skip.reshape(1, SSM_WIDTH), w_glu)


def _sb_kernel(*refs, tq, heads):
    q_refs, k_refs, v_refs = refs[:heads], refs[heads:2 * heads], refs[2 * heads:3 * heads]
    o_ref = refs[3 * heads]
    qi = pl.program_id(2)
    hd = HEAD_DIM_SB
    row = lax.broadcasted_iota(jnp.int32, (tq, tq), 0)
    col = lax.broadcasted_iota(jnp.int32, (tq, tq), 1)
    after = jnp.where(row > col, 1.0, 0.0).astype(jnp.bfloat16)
    after2 = jnp.concatenate([after, after], axis=0)
    causal = col < row
    dn = (((1,), (1,)), ((), ()))
    scale = hd ** -0.5
    log2e = math.log2(math.e)
    ln2 = math.log(2.0)

    def blocks(j, state, diagonal):
        start = pl.multiple_of(j * tq, tq)
        zs = []
        for h in range(heads):
            k = k_refs[h][pl.ds(start, tq), :]
            zs.append(lax.dot_general(qs[h], k, dn, preferred_element_type=jnp.float32) * scale)
        betweens, log_betas, sums = [], [], []
        for h in range(heads):
            z = zs[h]
            sp = jnp.maximum(z, 0.0) + jnp.log(1.0 + jnp.exp2(jnp.abs(z) * (-log2e)))
            log_betas.append(z - sp)
            if diagonal:
                sp = jnp.where(causal, sp, 0.0)
            hi = sp.astype(jnp.bfloat16)
            lo = (sp - hi.astype(jnp.float32)).astype(jnp.bfloat16)
            betweens.append(jnp.dot(jnp.concatenate([hi, lo], axis=1), after2,
                                    preferred_element_type=jnp.float32))
            sums.append(jnp.sum(sp, axis=-1, keepdims=True))
        out = []
        for h in range(heads):
            carry, acc = state[2 * h], state[2 * h + 1]
            w = jnp.exp2((log_betas[h] - betweens[h] - carry) * log2e)
            if diagonal:
                w = jnp.where(causal, w, 0.0)
            v = v_refs[h][pl.ds(start, tq), :]
            acc = acc + jnp.dot(w.astype(jnp.bfloat16), v, preferred_element_type=jnp.float32)
            out.extend((carry + sums[h], acc))
        return tuple(out)

    qs = [q_refs[h][...] for h in range(heads)]
    state = blocks(qi, (jnp.zeros((tq, 1), jnp.float32), jnp.zeros((tq, hd), jnp.float32)) * heads, True)

    def min_carry(state):
        m = state[0]
        for h in range(1, heads):
            m = jnp.minimum(m, state[2 * h])
        return jnp.min(m)

    def cond(loop):
        return jnp.logical_and(loop[0] < qi, loop[1] < SB_EXP_UNDERFLOW)

    def body(loop):
        state = blocks(qi - 1 - loop[0], loop[2:], False)
        return (loop[0] + 1, min_carry(state)) + state

    state = lax.while_loop(cond, body, (jnp.int32(0), min_carry(state)) + state)[2:]
    for h in range(heads):
        o_ref[:, h * hd:(h + 1) * hd] = state[2 * h + 1].astype(o_ref.dtype)


def _stick_breaking(proj, B, S):
    tq = _tile(S, 256, LANES)
    nq = S // tq
    heads = 4
    hd = HEAD_DIM_SB

    def head_specs(rows, off, row_index):
        return [pl.BlockSpec((rows, hd), functools.partial(
                    lambda b, g, i, s: (row_index(b, i), off // hd + g * heads + s), s=s))
                for s in range(heads)]

    return pl.pallas_call(
        functools.partial(_sb_kernel, tq=tq, heads=heads),
        grid=(B, N_SB // heads, nq),
        in_specs=(head_specs(tq, OFF_QS, lambda b, i: b * nq + i)
                  + head_specs(S, OFF_KS, lambda b, i: b)
                  + head_specs(S, OFF_VS, lambda b, i: b)),
        out_specs=pl.BlockSpec((tq, heads * hd), lambda b, g, i: (b * nq + i, g)),
        out_shape=jax.ShapeDtypeStruct((B * S, SB_W), jnp.bfloat16),
        compiler_params=_params("parallel", "parallel", "arbitrary"),
        name="stick_breaking",
    )(*([proj] * (3 * heads)))


def _merge_kernel(ya_ref, ys_ref, yc_ref, w_ref, *rest):
    o_ref = rest[-1]
    g_refs = rest[:-1]
    subs = len(g_refs) // N_BRANCH
    acc = None
    for k, y_ref in enumerate((ya_ref, ys_ref, yc_ref)):
        t = jnp.dot(y_ref[...], w_ref[k], preferred_element_type=jnp.float32)
        g = jnp.concatenate([g_refs[k * subs + s][...] for s in range(subs)], axis=-1)
        t = _sigmoid(g.astype(jnp.float32)) * t
        acc = t if acc is None else acc + t
    o_ref[...] = acc.astype(o_ref.dtype)


def _merge(y_a, y_s, y_c, w_branch, l, proj, D):
    M = y_a.shape[0]
    tm = _tile(M, 1024, BF16_SUBLANES)
    tn = _tile(D, 512, COL_BLOCK)
    subs = tn // COL_BLOCK
    gate_specs = [pl.BlockSpec((tm, COL_BLOCK), functools.partial(
                      lambda m, n, k, s: (m, (OFF_G + k * D) // COL_BLOCK + n * subs + s), k=k, s=s))
                  for k in range(N_BRANCH) for s in range(subs)]
    row = pl.BlockSpec((tm, BRANCH_WIDTH), lambda m, n: (m, 0))
    return pl.pallas_call(
        _merge_kernel,
        grid=(M // tm, D // tn),
        in_specs=[row, row, row,
                  pl.BlockSpec((None, N_BRANCH, BRANCH_WIDTH, tn), lambda m, n: (l, 0, 0, n))] + gate_specs,
        out_specs=pl.BlockSpec((tm, tn), lambda m, n: (m, n)),
        out_shape=jax.ShapeDtypeStruct((M, D), jnp.bfloat16),
        compiler_params=_params("parallel", "arbitrary"),
        name="gated_merge",
    )(y_a, y_s, y_c, w_branch, *([proj] * len(gate_specs)))


def _ffn_up_kernel(h_ref, halo_ref, wa_ref, wb_ref, cwa_ref, cwb_ref, cba_ref, cbb_ref,
                   o_ref, a_scr, *, tm, tiles_per_seq):
    H = BF16_SUBLANES

    @pl.when(pl.program_id(1) == 0)
    def _():
        first = pl.program_id(0) % tiles_per_seq == 0
        halo = halo_ref[...]
        a_scr[0:H, :] = jnp.where(first, jnp.zeros_like(halo), halo)
        a_scr[H:, :] = h_ref[...]

    a = a_scr[...]

    def conv(w_ref, cw_ref, cb_ref):
        u = jnp.dot(a, w_ref[...], preferred_element_type=jnp.float32)
        u1 = pltpu.roll(u, 1, 0)
        u2 = pltpu.roll(u, 2, 0)
        return (cb_ref[...] + cw_ref[0:1, :] * u2[H:] + cw_ref[1:2, :] * u1[H:]
                + cw_ref[2:3, :] * u[H:])

    ga = conv(wa_ref, cwa_ref, cba_ref)
    gb = conv(wb_ref, cwb_ref, cbb_ref)
    o_ref[...] = (_gelu_tanh(ga) * gb).astype(o_ref.dtype)


def _ffn_up(h, w_up, l, conv_w, conv_b, S):
    M, D = h.shape
    F = w_up.shape[2] // 2
    H = BF16_SUBLANES
    tm = _tile(S, 1024, H)
    tn = _tile(F, 512, LANES)
    nf = F // tn
    return pl.pallas_call(
        functools.partial(_ffn_up_kernel, tm=tm, tiles_per_seq=S // tm),
        grid=(M // tm, nf),
        in_specs=[pl.BlockSpec((tm, D), lambda m, n: (m, 0)),
                  pl.BlockSpec((H, D), lambda m, n: (jnp.maximum(m * (tm // H) - 1, 0), 0)),
                  pl.BlockSpec((None, D, tn), lambda m, n: (l, 0, n)),
                  pl.BlockSpec((None, D, tn), lambda m, n: (l, 0, nf + n)),
                  pl.BlockSpec((CONV_W, tn), lambda m, n: (0, n)),
                  pl.BlockSpec((CONV_W, tn), lambda m, n: (0, nf + n)),
                  pl.BlockSpec((1, tn), lambda m, n: (0, n)),
                  pl.BlockSpec((1, tn), lambda m, n: (0, nf + n))],
        out_specs=pl.BlockSpec((tm, tn), lambda m, n: (m, n)),
        out_shape=jax.ShapeDtypeStruct((M, F), jnp.bfloat16),
        scratch_shapes=[pltpu.VMEM((H + tm, D), jnp.bfloat16)],
        compiler_params=_params("parallel", "arbitrary"),
        name="ffn_up_conv_gate",
    )(h, h, w_up, w_up, conv_w, conv_w, conv_b.reshape(1, 2 * F), conv_b.reshape(1, 2 * F))


def kernel(x, c, w_ada, b_ada, ada_table, norm_mix_pre, norm_mix_post, norm_ffn_pre, norm_ffn_post,
           w_in, attn_sinks, ssm_a_re, ssm_a_im, ssm_log_dt, ssm_b_re, ssm_b_im, ssm_c_re, ssm_c_im,
           ssm_d, ssm_w_glu, w_branch, w_out, ffn_w_up, ffn_conv_w, ffn_conv_b, ffn_w_down):
    B, S, D = x.shape
    depth = w_in.shape[0]
    T = B * S
    bf16 = jnp.bfloat16

    w_in_l = w_in[:1].astype(bf16)
    w_glu_b = ssm_w_glu.astype(bf16)
    w_branch_rows = w_branch.reshape(depth, N_BRANCH * BRANCH_WIDTH, D)

    mod_shared = _ada_proj(c, w_ada, b_ada)
    h = _normmod(x, norm_mix_pre[0], mod_shared, ada_table[0], 0)
    for l in range(depth):
        tab = ada_table[l]
        casts = [(ffn_w_up, l), (ffn_w_down, l), (w_out, l), (w_branch_rows, l)]
        if l + 1 < depth:
            casts.append((w_in, l + 1))
        proj, cast = _matmul(h.reshape(T, D), w_in_l, 0, bf16, 1024, 768, "in_proj", casts=casts)
        w_up_l, w_down_l, w_out_l, w_branch_l = cast[:4]
        w_branch_l = w_branch_l.reshape(1, N_BRANCH, BRANCH_WIDTH, D)
        if l + 1 < depth:
            w_in_l = cast[4]
        y_a = _swa(proj, attn_sinks[l], B, S)
        tables = _ssm_tables(ssm_a_re[l], ssm_a_im[l], ssm_log_dt[l], ssm_b_re[l], ssm_b_im[l],
                             ssm_c_re[l], ssm_c_im[l], B)
        y_s = _ssm(proj, tables, ssm_d[l], w_glu_b, l, B, S).reshape(T, SSM_WIDTH)
        y_c = _stick_breaking(proj, B, S)
        merged = _merge(y_a, y_s, y_c, w_branch_l, 0, proj, D)
        y = _matmul(merged, w_out_l, 0, bf16, 1024, 1024, "out_proj")
        x, h = _postnorm_residual(x, y, norm_mix_post[l], mod_shared, tab, 2,
                                  next_norm=(norm_ffn_pre[l], tab, 3))

        gact = _ffn_up(h.reshape(T, D), w_up_l, 0, ffn_conv_w[l], ffn_conv_b[l], S)
        y = _matmul(gact, w_down_l, 0, bf16, 1024, 512, "ffn_down")
        if l + 1 < depth:
            x, h = _postnorm_residual(x, y, norm_ffn_post[l], mod_shared, tab, 5,
                                      next_norm=(norm_mix_pre[l + 1], ada_table[l + 1], 0))
        else:
            x = _postnorm_residual(x, y, norm_ffn_post[l], mod_shared, tab, 5)
    return x
```

```python
import functools
import math

import jax
import jax.numpy as jnp
from jax import lax
from jax.experimental import pallas as pl
from jax.experimental.pallas import tpu as pltpu

HEAD_DIM_A = 64
N_Q_A = 16
N_KV_A = 2
WINDOW = 128
SSM_WIDTH = 1024
SSM_GROUP = 16
SSM_GROUPS = SSM_WIDTH // SSM_GROUP
SSM_STATE = 64
N_SB = 8
HEAD_DIM_SB = 128
BRANCH_WIDTH = 1024
N_BRANCH = 3
CONV_W = 3
N_MOD = 6
EPS = 1e-6
NEG_INF = -1e30

SB_EXP_UNDERFLOW = 105.0

Q_A = N_Q_A * HEAD_DIM_A
KV_A = N_KV_A * HEAD_DIM_A
SB_W = N_SB * HEAD_DIM_SB

OFF_QA = 0
OFF_KA = OFF_QA + Q_A
OFF_VA = OFF_KA + KV_A
OFF_US = OFF_VA + KV_A
OFF_QS = OFF_US + SSM_WIDTH
OFF_KS = OFF_QS + SB_W
OFF_VS = OFF_KS + SB_W
OFF_G = OFF_VS + SB_W
COL_BLOCK = 256

LANES = 128
BF16_SUBLANES = 16
POSTNORM_CHUNK = BF16_SUBLANES
VMEM_LIMIT_BYTES = 56 * 1024 * 1024

SSM_GROUPS_PER_BLOCK = LANES // SSM_GROUP
SSM_BLOCKS = SSM_GROUPS // SSM_GROUPS_PER_BLOCK
SSM_STATES_PER_BLOCK = SSM_GROUPS_PER_BLOCK * SSM_STATE
SSM_STATES = SSM_GROUPS * SSM_STATE
SSM_STATE_TILES = SSM_STATES // LANES
SSM_PACK = 2


def _tile(dim, pref, align):
    t = (min(pref, dim) // align) * align
    while t >= align:
        if dim % t == 0:
            return t
        t -= align
    return dim


def _params(*sem):
    return pltpu.CompilerParams(dimension_semantics=sem, vmem_limit_bytes=VMEM_LIMIT_BYTES)


def _gelu_tanh(x):
    return 0.5 * x * (1.0 + jnp.tanh(math.sqrt(2.0 / math.pi) * (x + 0.044715 * (x * x * x))))


def _sigmoid(x):
    return 1.0 / (1.0 + jnp.exp(-x))


def _ada_kernel(c_ref, w_ref, b_ref, o_ref):
    c = c_ref[...]
    a = (c * _sigmoid(c)).astype(jnp.bfloat16)
    o_ref[...] = jnp.dot(a, w_ref[...].astype(jnp.bfloat16),
                         preferred_element_type=jnp.float32) + b_ref[...]


def _ada_proj(c, w_ada, b_ada):
    B, D = c.shape
    N = w_ada.shape[1]
    rows = 8
    c_pad = jnp.zeros((rows, D), jnp.float32).at[:B].set(c)
    tn = _tile(N, 512, LANES)
    out = pl.pallas_call(
        _ada_kernel,
        grid=(N // tn,),
        in_specs=[pl.BlockSpec((rows, D), lambda n: (0, 0)),
                  pl.BlockSpec((D, tn), lambda n: (0, n)),
                  pl.BlockSpec((1, tn), lambda n: (0, n))],
        out_specs=pl.BlockSpec((rows, tn), lambda n: (0, n)),
        out_shape=jax.ShapeDtypeStruct((rows, N), jnp.float32),
        compiler_params=_params("parallel"),
        name="ada_proj",
    )(c_pad, w_ada, b_ada.reshape(1, N))
    return out[:B].reshape(B, N_MOD, D)


def _normmod_kernel(x_ref, g_ref, mod_ref, tab_ref, h_ref, *, shift_idx):
    x = x_ref[0]
    ms = jnp.mean(x * x, axis=-1, keepdims=True)
    y = x * lax.rsqrt(ms + EPS) * g_ref[...]
    shift = mod_ref[0, shift_idx:shift_idx + 1, :] + tab_ref[shift_idx:shift_idx + 1, :]
    scale = mod_ref[0, shift_idx + 1:shift_idx + 2, :] + tab_ref[shift_idx + 1:shift_idx + 2, :]
    h_ref[0] = (y * (1.0 + scale) + shift).astype(h_ref.dtype)


def _normmod(x, g, mod_shared, tab, shift_idx):
    B, S, D = x.shape
    tr = _tile(S, 256, BF16_SUBLANES)
    return pl.pallas_call(
        functools.partial(_normmod_kernel, shift_idx=shift_idx),
        grid=(B, S // tr),
        in_specs=[pl.BlockSpec((1, tr, D), lambda b, i: (b, i, 0)),
                  pl.BlockSpec((1, D), lambda b, i: (0, 0)),
                  pl.BlockSpec((1, N_MOD, D), lambda b, i: (b, 0, 0)),
                  pl.BlockSpec((N_MOD, D), lambda b, i: (0, 0))],
        out_specs=pl.BlockSpec((1, tr, D), lambda b, i: (b, i, 0)),
        out_shape=jax.ShapeDtypeStruct((B, S, D), jnp.bfloat16),
        compiler_params=_params("parallel", "parallel"),
        name="normmod",
    )(x, g.reshape(1, D), mod_shared, tab)


def _postnorm_kernel(x_ref, y_ref, g_ref, mod_ref, tab_ref, *rest, gate_idx, next_shift_idx):
    gate = mod_ref[0, gate_idx:gate_idx + 1, :] + tab_ref[gate_idx:gate_idx + 1, :]
    if next_shift_idx is None:
        (o_ref,) = rest
    else:
        gn_ref, tabn_ref, o_ref, h_ref = rest
        k = next_shift_idx
        shift = mod_ref[0, k:k + 1, :] + tabn_ref[k:k + 1, :]
        scale1 = 1.0 + (mod_ref[0, k + 1:k + 2, :] + tabn_ref[k + 1:k + 2, :])

    def chunk(i, carry):
        rows = pl.ds(pl.multiple_of(i * POSTNORM_CHUNK, POSTNORM_CHUNK), POSTNORM_CHUNK)
        y = y_ref[0, rows, :].astype(jnp.float32)
        ms = jnp.mean(y * y, axis=-1, keepdims=True)
        yn = y * lax.rsqrt(ms + EPS) * g_ref[...]
        x = x_ref[0, rows, :] + gate * yn
        o_ref[0, rows, :] = x
        if next_shift_idx is not None:
            ms = jnp.mean(x * x, axis=-1, keepdims=True)
            xn = x * lax.rsqrt(ms + EPS) * gn_ref[...]
            h_ref[0, rows, :] = (xn * scale1 + shift).astype(h_ref.dtype)
        return carry

    lax.fori_loop(0, x_ref.shape[1] // POSTNORM_CHUNK, chunk, 0, unroll=4)


def _postnorm_residual(x, y, g, mod_shared, tab, gate_idx, next_norm=None):
    B, S, D = x.shape
    tr = _tile(S, 512, POSTNORM_CHUNK)
    row = pl.BlockSpec((1, tr, D), lambda b, i: (b, i, 0))
    vec = pl.BlockSpec((1, D), lambda b, i: (0, 0))
    tabs = pl.BlockSpec((N_MOD, D), lambda b, i: (0, 0))
    in_specs = [row, row, vec, pl.BlockSpec((1, N_MOD, D), lambda b, i: (b, 0, 0)), tabs]
    args = [x, y.reshape(B, S, D), g.reshape(1, D), mod_shared, tab]
    x_shape = jax.ShapeDtypeStruct((B, S, D), jnp.float32)
    if next_norm is None:
        out_specs, out_shape, next_shift_idx = row, x_shape, None
    else:
        g_next, tab_next, next_shift_idx = next_norm
        in_specs += [vec, tabs]
        args += [g_next.reshape(1, D), tab_next]
        out_specs = (row, row)
        out_shape = (x_shape, jax.ShapeDtypeStruct((B, S, D), jnp.bfloat16))
    return pl.pallas_call(
        functools.partial(_postnorm_kernel, gate_idx=gate_idx, next_shift_idx=next_shift_idx),
        grid=(B, S // tr),
        in_specs=in_specs,
        out_specs=out_specs,
        out_shape=out_shape,
        compiler_params=_params("parallel", "parallel"),
        name="postnorm_residual",
    )(*args)


def _mm_kernel(a_ref, w_ref, *rest):
    n_casts = len(rest) // 2
    o_ref = rest[n_casts]
    o_ref[...] = jnp.dot(a_ref[...], w_ref[...],
                         preferred_element_type=jnp.float32).astype(o_ref.dtype)
    for src_ref, dst_ref in zip(rest[:n_casts], rest[n_casts + 1:]):
        dst_ref[...] = src_ref[...].astype(dst_ref.dtype)


def _matmul(a, w, l, out_dtype, tm_pref, tn_pref, name, casts=()):
    M, K = a.shape
    N = w.shape[2]
    tm = _tile(M, tm_pref, BF16_SUBLANES)
    tn = _tile(N, tn_pref, LANES)
    gm, gn = M // tm, N // tn
    in_specs = [pl.BlockSpec((tm, K), lambda m, n: (m, 0)),
                pl.BlockSpec((None, K, tn), lambda m, n: (l, 0, n))]
    out_specs = [pl.BlockSpec((tm, tn), lambda m, n: (m, n))]
    out_shape = [jax.ShapeDtypeStruct((M, N), out_dtype)]
    args = [a, w]
    for src, l_src in casts:
        _, R, C = src.shape
        rb = next(r for r in range(BF16_SUBLANES, R + 1, BF16_SUBLANES)
                  if R % r == 0 and R // r <= gm * gn)
        last = R // rb - 1
        block = functools.partial(lambda m, n, last: jnp.minimum(m * gn + n, last), last=last)
        in_specs.append(pl.BlockSpec((None, rb, C), functools.partial(
            lambda m, n, block, l_src: (l_src, block(m, n), 0), block=block, l_src=l_src)))
        out_specs.append(pl.BlockSpec((rb, C), functools.partial(
            lambda m, n, block: (block(m, n), 0), block=block)))
        out_shape.append(jax.ShapeDtypeStruct((R, C), jnp.bfloat16))
        args.append(src)
    outs = pl.pallas_call(
        _mm_kernel,
        grid=(gm, gn),
        in_specs=in_specs,
        out_specs=out_specs,
        out_shape=out_shape,
        compiler_params=_params("arbitrary", "arbitrary"),
        name=name,
    )(*args)
    if not casts:
        return outs[0]
    return outs[0], [o[None] for o in outs[1:]]


def _swa_kernel(sink_ref, biasp_ref, biasc_ref, q_ref, kp_ref, kc_ref, vp_ref, vc_ref, o_ref):
    i = pl.program_id(1)
    W = WINDOW
    Dh = HEAD_DIM_A
    G = N_Q_A // N_KV_A
    n_pairs = N_Q_A // 2
    dn = (((1,), (1,)), ((), ()))
    zeros = jnp.zeros((W, Dh), jnp.bfloat16)

    def blockdiag(x):
        return jnp.concatenate([jnp.concatenate([x, zeros], axis=1),
                                jnp.concatenate([zeros, x], axis=1)], axis=0)

    kbd_p, kbd_c, vbd_p, vbd_c = [], [], [], []
    for hk in range(N_KV_A):
        sl = slice(hk * Dh, (hk + 1) * Dh)
        kbd_p.append(blockdiag(kp_ref[:, sl]))
        kbd_c.append(blockdiag(kc_ref[:, sl]))
        vbd_p.append(blockdiag(vp_ref[:, sl]))
        vbd_c.append(blockdiag(vc_ref[:, sl]))

    no_prev = jnp.where(i > 0, 0.0, NEG_INF)
    scores = []
    for p in range(n_pairs):
        hk = (2 * p) // G
        q = q_ref[:, p * 2 * Dh:(p + 1) * 2 * Dh]
        scores.append((lax.dot_general(q, kbd_p[hk], dn, preferred_element_type=jnp.float32),
                       lax.dot_general(q, kbd_c[hk], dn, preferred_element_type=jnp.float32)))

    lane = lax.broadcasted_iota(jnp.int32, (W, 2 * Dh), 1)
    for p in range(n_pairs):
        hk = (2 * p) // G
        tp = scores[p][0] * (Dh ** -0.5) + (biasp_ref[p] + no_prev)
        tc = scores[p][1] * (Dh ** -0.5) + biasc_ref[p]
        eps, ecs, invs = [], [], []
        for half in range(2):
            sink = sink_ref[2 * p + half]
            hl = slice(half * W, (half + 1) * W)
            m = jnp.maximum(jnp.max(jnp.maximum(tp[:, hl], tc[:, hl]), axis=-1, keepdims=True), sink)
            ep = jnp.exp(tp[:, hl] - m)
            ec = jnp.exp(tc[:, hl] - m)
            denom = jnp.sum(ep + ec, axis=-1, keepdims=True) + jnp.exp(sink - m)
            eps.append(ep.astype(jnp.bfloat16))
            ecs.append(ec.astype(jnp.bfloat16))
            invs.append(1.0 / denom)
        o = (jnp.dot(jnp.concatenate(eps, axis=1), vbd_p[hk], preferred_element_type=jnp.float32)
             + jnp.dot(jnp.concatenate(ecs, axis=1), vbd_c[hk], preferred_element_type=jnp.float32))
        o = o * jnp.where(lane < Dh, invs[0], invs[1])
        o_ref[:, p * 2 * Dh:(p + 1) * 2 * Dh] = o.astype(o_ref.dtype)


def _swa_bias():
    W = WINDOW
    r = jnp.arange(W)[:, None]
    c = jnp.arange(W)[None, :]
    slopes = jnp.exp2(-8.0 * jnp.arange(1, N_Q_A + 1, dtype=jnp.float32) / N_Q_A)[:, None, None]
    bp = jnp.where(c > r, -slopes * (r + W - c).astype(jnp.float32), NEG_INF)
    bc = jnp.where(c <= r, -slopes * (r - c).astype(jnp.float32), NEG_INF)
    pair = lambda b: b.reshape(N_Q_A // 2, 2, W, W).transpose(0, 2, 1, 3).reshape(N_Q_A // 2, W, 2 * W)
    return pair(bp), pair(bc)


def _swa(proj, sinks, B, S):
    nb = S // WINDOW
    kb = OFF_KA // KV_A
    vb = OFF_VA // KV_A
    cur = lambda col: (lambda b, i: (b * nb + i, col))
    prev = lambda col: (lambda b, i: (b * nb + jnp.maximum(i - 1, 0), col))
    biasp, biasc = _swa_bias()
    bias_spec = pl.BlockSpec(biasp.shape, lambda b, i: (0, 0, 0))
    return pl.pallas_call(
        _swa_kernel,
        grid=(B, nb),
        in_specs=[pl.BlockSpec(memory_space=pltpu.SMEM), bias_spec, bias_spec,
                  pl.BlockSpec((WINDOW, Q_A), cur(OFF_QA // Q_A)),
                  pl.BlockSpec((WINDOW, KV_A), prev(kb)),
                  pl.BlockSpec((WINDOW, KV_A), cur(kb)),
                  pl.BlockSpec((WINDOW, KV_A), prev(vb)),
                  pl.BlockSpec((WINDOW, KV_A), cur(vb))],
        out_specs=pl.BlockSpec((WINDOW, Q_A), lambda b, i: (b * nb + i, 0)),
        out_shape=jax.ShapeDtypeStruct((B * S, Q_A), jnp.bfloat16),
        compiler_params=_params("parallel", "parallel"),
        name="swa",
    )(sinks, biasp, biasc, proj, proj, proj, proj, proj)


def _ssm_kernel(*refs, B, Lc):
    n_u = SSM_WIDTH // COL_BLOCK
    u_refs = refs[:n_u]
    wbr_ref, wbi_ref, a1_ref, a2_ref, cr_ref, ci_ref, d_ref, wg_ref, o_ref, s_ref, x_ref = refs[n_u:]

    @pl.when(pl.program_id(0) == 0)
    def _():
        x_ref[...] = jnp.zeros_like(x_ref)

    R = B * Lc
    u = jnp.concatenate([r[...] for r in u_refs], axis=-1).reshape(R, SSM_WIDTH)
    lpb = SSM_STATES_PER_BLOCK // LANES

    shape = x_ref.shape
    sub = shape[1]

    def time_rows(tile, part, b):
        return pl.ds(((tile % SSM_PACK) * 2 + part) * B + b, Lc, stride=sub)

    for j in range(SSM_BLOCKS):
        uj = u[:, j * LANES:(j + 1) * LANES]
        drive = (jnp.dot(uj, wbr_ref[j], preferred_element_type=jnp.float32),
                 jnp.dot(uj, wbi_ref[j], preferred_element_type=jnp.float32))
        for k in range(lpb):
            tile = j * lpb + k
            for part in range(2):
                for b in range(B):
                    s_ref[tile // SSM_PACK, time_rows(tile, part, b), :] = (
                        drive[part][b * Lc:(b + 1) * Lc, k * LANES:(k + 1) * LANES])

    is_re = lax.broadcasted_iota(jnp.int32, shape, 1) % (2 * B) < B

    def step(t, x):
        rows = pl.ds(pl.multiple_of(t * sub, sub), sub)
        swapped = jnp.where(is_re, pltpu.roll(x, sub - B, 1), pltpu.roll(x, B, 1))
        new = a1_ref[...] * x + a2_ref[...] * swapped + s_ref[:, rows, :]
        s_ref[:, rows, :] = new
        return new

    x_ref[...] = lax.fori_loop(0, Lc, step, x_ref[...], unroll=4)

    def states(j, part):
        return jnp.concatenate(
            [jnp.concatenate([s_ref[t // SSM_PACK, time_rows(t, part, b), :] for b in range(B)], axis=0)
             for t in range(j * lpb, (j + 1) * lpb)], axis=-1).astype(jnp.bfloat16)

    ys = []
    for j in range(SSM_BLOCKS):
        ys.append(jnp.dot(states(j, 0), cr_ref[j], preferred_element_type=jnp.float32)
                  + jnp.dot(states(j, 1), ci_ref[j], preferred_element_type=jnp.float32))
    y = jnp.concatenate(ys, axis=-1) + d_ref[...] * u.astype(jnp.float32)
    y = _gelu_tanh(y).astype(jnp.bfloat16)
    vg = jnp.dot(y, wg_ref[...], preferred_element_type=jnp.float32)
    out = vg[:, :SSM_WIDTH] * _sigmoid(vg[:, SSM_WIDTH:])
    o_ref[...] = out.reshape(B, Lc, SSM_WIDTH).astype(o_ref.dtype)


def _ssm_tables(a_re, a_im, log_dt, b_re, b_im, c_re, c_im, batch):
    f32 = jnp.float32
    lr, li = a_re.astype(f32), a_im.astype(f32)
    dt = jnp.exp(log_dt.astype(f32))[:, None]
    mag = jnp.exp(lr * dt)
    abr, abi = mag * jnp.cos(li * dt), mag * jnp.sin(li * dt)
    den = lr * lr + li * li
    kr = ((abr - 1.0) * lr + abi * li) / den
    ki = (abi * lr - (abr - 1.0) * li) / den
    br, bi = b_re.astype(f32), b_im.astype(f32)
    bbr = kr[..., None] * br - ki[..., None] * bi
    bbi = kr[..., None] * bi + ki[..., None] * br
    gpb, P, N = SSM_GROUPS_PER_BLOCK, SSM_GROUP, SSM_STATE
    eye = jnp.eye(gpb, dtype=f32)

    def in_blocks(w):
        w = w.reshape(SSM_BLOCKS, gpb, N, P)
        return jnp.einsum('jgnp,gh->jgphn', w, eye).reshape(SSM_BLOCKS, gpb * P, gpb * N)

    def out_blocks(w):
        w = w.reshape(SSM_BLOCKS, gpb, P, N)
        return jnp.einsum('jgpn,gh->jgnhp', w, eye).reshape(SSM_BLOCKS, gpb * N, gpb * P)

    wbr = in_blocks(bbr).astype(jnp.bfloat16)
    wbi = in_blocks(bbi).astype(jnp.bfloat16)
    cr = out_blocks(c_re.astype(f32)).astype(jnp.bfloat16)
    ci = out_blocks(-c_im.astype(f32)).astype(jnp.bfloat16)
    pairs = SSM_STATE_TILES // SSM_PACK
    a1 = jnp.broadcast_to(abr.reshape(pairs, SSM_PACK, 1, 1, LANES), (pairs, SSM_PACK, 2, batch, LANES))
    sign = jnp.array([-1.0, 1.0], f32).reshape(1, 1, 2, 1, 1)
    a2 = jnp.broadcast_to(abi.reshape(pairs, SSM_PACK, 1, 1, LANES) * sign, (pairs, SSM_PACK, 2, batch, LANES))
    rows = SSM_PACK * 2 * batch
    return wbr, wbi, a1.reshape(pairs, rows, LANES), a2.reshape(pairs, rows, LANES), cr, ci


def _ssm(proj, tables, d_skip, w_glu, l, B, S):
    wbr, wbi, a1, a2, cr, ci = tables
    n_in = proj.shape[1]
    Lc = _tile(S, 256, BF16_SUBLANES)
    rows = SSM_PACK * 2 * B
    assert rows == 8, "packed scan layout needs (lane tiles per vreg) * 2 * batch == 8 sublanes"
    pairs = SSM_STATE_TILES // SSM_PACK
    full = lambda *shape: pl.BlockSpec(shape, lambda c: (0,) * len(shape))
    n_u = SSM_WIDTH // COL_BLOCK
    u_specs = [pl.BlockSpec((B, Lc, COL_BLOCK), functools.partial(lambda c, s: (0, c, OFF_US // COL_BLOCK + s), s=s))
               for s in range(n_u)]
    proj3 = proj.reshape(B, S, n_in)
    return pl.pallas_call(
        functools.partial(_ssm_kernel, B=B, Lc=Lc),
        grid=(S // Lc,),
        in_specs=u_specs + [full(*wbr.shape), full(*wbi.shape), full(*a1.shape), full(*a2.shape),
                            full(*cr.shape), full(*ci.shape), full(1, SSM_WIDTH),
                            pl.BlockSpec((None,) + w_glu.shape[1:], lambda c: (l, 0, 0))],
        out_specs=pl.BlockSpec((B, Lc, SSM_WIDTH), lambda c: (0, c, 0)),
        out_shape=jax.ShapeDtypeStruct((B, S, SSM_WIDTH), jnp.bfloat16),
        scratch_shapes=[pltpu.VMEM((pairs, rows * Lc, LANES), jnp.float32),
                        pltpu.VMEM((pairs, rows, LANES), jnp.float32)],
        compiler_params=_params("arbitrary"),
        name="s5_ssm",
    )(*([proj3] * n_u), wbr, wbi, a1, a2, cr, ci, d_skip.reshape(1, SSM_WIDTH), w_glu)


def _sb_kernel(*refs, tq, heads):
    q_refs, k_refs, v_refs = refs[:heads], refs[heads:2 * heads], refs[2 * heads:3 * heads]
    o_ref = refs[3 * heads]
    qi = pl.program_id(2)
    hd = HEAD_DIM_SB
    row = lax.broadcasted_iota(jnp.int32, (tq, tq), 0)
    col = lax.broadcasted_iota(jnp.int32, (tq, tq), 1)
    after = jnp.where(row > col, 1.0, 0.0).astype(jnp.bfloat16)
    after2 = jnp.concatenate([after, after], axis=0)
    causal = col < row
    dn = (((1,), (1,)), ((), ()))
    scale = hd ** -0.5
    log2e = math.log2(math.e)
    ln2 = math.log(2.0)

    def blocks(j, state, diagonal):
        start = pl.multiple_of(j * tq, tq)
        zs = []
        for h in range(heads):
            k = k_refs[h][pl.ds(start, tq), :]
            zs.append(lax.dot_general(qs[h], k, dn, preferred_element_type=jnp.float32) * scale)
        betweens, log_betas, sums = [], [], []
        for h in range(heads):
            z = zs[h]
            sp = jnp.maximum(z, 0.0) + jnp.log(1.0 + jnp.exp2(jnp.abs(z) * (-log2e)))
            log_betas.append(z - sp)
            if diagonal:
                sp = jnp.where(causal, sp, 0.0)
            hi = sp.astype(jnp.bfloat16)
            lo = (sp - hi.astype(jnp.float32)).astype(jnp.bfloat16)
            betweens.append(jnp.dot(jnp.concatenate([hi, lo], axis=1), after2,
                                    preferred_element_type=jnp.float32))
            sums.append(jnp.sum(sp, axis=-1, keepdims=True))
        out = []
        for h in range(heads):
            carry, acc = state[2 * h], state[2 * h + 1]
            w = jnp.exp2((log_betas[h] - betweens[h] - carry) * log2e)
            if diagonal:
                w = jnp.where(causal, w, 0.0)
            v = v_refs[h][pl.ds(start, tq), :]
            acc = acc + jnp.dot(w.astype(jnp.bfloat16), v, preferred_element_type=jnp.float32)
            out.extend((carry + sums[h], acc))
        return tuple(out)

    qs = [q_refs[h][...] for h in range(heads)]
    state = blocks(qi, (jnp.zeros((tq, 1), jnp.float32), jnp.zeros((tq, hd), jnp.float32)) * heads, True)

    def min_carry(state):
        m = state[0]
        for h in range(1, heads):
            m = jnp.minimum(m, state[2 * h])
        return jnp.min(m)

    def cond(loop):
        return jnp.logical_and(loop[0] < qi, loop[1] < SB_EXP_UNDERFLOW)

    def body(loop):
        state = blocks(qi - 1 - loop[0], loop[2:], False)
        return (loop[0] + 1, min_carry(state)) + state

    state = lax.while_loop(cond, body, (jnp.int32(0), min_carry(state)) + state)[2:]
    for h in range(heads):
        o_ref[:, h * hd:(h + 1) * hd] = state[2 * h + 1].astype(o_ref.dtype)


def _stick_breaking(proj, B, S):
    tq = _tile(S, 256, LANES)
    nq = S // tq
    heads = 4
    hd = HEAD_DIM_SB

    def head_specs(rows, off, row_index):
        return [pl.BlockSpec((rows, hd), functools.partial(
                    lambda b, g, i, s: (row_index(b, i), off // hd + g * heads + s), s=s))
                for s in range(heads)]

    return pl.pallas_call(
        functools.partial(_sb_kernel, tq=tq, heads=heads),
        grid=(B, N_SB // heads, nq),
        in_specs=(head_specs(tq, OFF_QS, lambda b, i: b * nq + i)
                  + head_specs(S, OFF_KS, lambda b, i: b)
                  + head_specs(S, OFF_VS, lambda b, i: b)),
        out_specs=pl.BlockSpec((tq, heads * hd), lambda b, g, i: (b * nq + i, g)),
        out_shape=jax.ShapeDtypeStruct((B * S, SB_W), jnp.bfloat16),
        compiler_params=_params("parallel", "parallel", "arbitrary"),
        name="stick_breaking",
    )(*([proj] * (3 * heads)))


def _merge_kernel(ya_ref, ys_ref, yc_ref, w_ref, *rest):
    o_ref = rest[-1]
    g_refs = rest[:-1]
    subs = len(g_refs) // N_BRANCH
    acc = None
    for k, y_ref in enumerate((ya_ref, ys_ref, yc_ref)):
        t = jnp.dot(y_ref[...], w_ref[k], preferred_element_type=jnp.float32)
        g = jnp.concatenate([g_refs[k * subs + s][...] for s in range(subs)], axis=-1)
        t = _sigmoid(g.astype(jnp.float32)) * t
        acc = t if acc is None else acc + t
    o_ref[...] = acc.astype(o_ref.dtype)


def _merge(y_a, y_s, y_c, w_branch, l, proj, D):
    M = y_a.shape[0]
    tm = _tile(M, 1024, BF16_SUBLANES)
    tn = _tile(D, 512, COL_BLOCK)
    subs = tn // COL_BLOCK
    gate_specs = [pl.BlockSpec((tm, COL_BLOCK), functools.partial(
                      lambda m, n, k, s: (m, (OFF_G + k * D) // COL_BLOCK + n * subs + s), k=k, s=s))
                  for k in range(N_BRANCH) for s in range(subs)]
    row = pl.BlockSpec((tm, BRANCH_WIDTH), lambda m, n: (m, 0))
    return pl.pallas_call(
        _merge_kernel,
        grid=(M // tm, D // tn),
        in_specs=[row, row, row,
                  pl.BlockSpec((None, N_BRANCH, BRANCH_WIDTH, tn), lambda m, n: (l, 0, 0, n))] + gate_specs,
        out_specs=pl.BlockSpec((tm, tn), lambda m, n: (m, n)),
        out_shape=jax.ShapeDtypeStruct((M, D), jnp.bfloat16),
        compiler_params=_params("parallel", "arbitrary"),
        name="gated_merge",
    )(y_a, y_s, y_c, w_branch, *([proj] * len(gate_specs)))


def _ffn_up_kernel(h_ref, halo_ref, wa_ref, wb_ref, cwa_ref, cwb_ref, cba_ref, cbb_ref,
                   o_ref, a_scr, *, tm, tiles_per_seq):
    H = BF16_SUBLANES

    @pl.when(pl.program_id(1) == 0)
    def _():
        first = pl.program_id(0) % tiles_per_seq == 0
        halo = halo_ref[...]
        a_scr[0:H, :] = jnp.where(first, jnp.zeros_like(halo), halo)
        a_scr[H:, :] = h_ref[...]

    a = a_scr[...]

    def conv(w_ref, cw_ref, cb_ref):
        u = jnp.dot(a, w_ref[...], preferred_element_type=jnp.float32)
        u1 = pltpu.roll(u, 1, 0)
        u2 = pltpu.roll(u, 2, 0)
        return (cb_ref[...] + cw_ref[0:1, :] * u2[H:] + cw_ref[1:2, :] * u1[H:]
                + cw_ref[2:3, :] * u[H:])

    ga = conv(wa_ref, cwa_ref, cba_ref)
    gb = conv(wb_ref, cwb_ref, cbb_ref)
    o_ref[...] = (_gelu_tanh(ga) * gb).astype(o_ref.dtype)


def _ffn_up(h, w_up, l, conv_w, conv_b, S):
    M, D = h.shape
    F = w_up.shape[2] // 2
    H = BF16_SUBLANES
    tm = _tile(S, 1024, H)
    tn = _tile(F, 512, LANES)
    nf = F // tn
    return pl.pallas_call(
        functools.partial(_ffn_up_kernel, tm=tm, tiles_per_seq=S // tm),
        grid=(M // tm, nf),
        in_specs=[pl.BlockSpec((tm, D), lambda m, n: (m, 0)),
                  pl.BlockSpec((H, D), lambda m, n: (jnp.maximum(m * (tm // H) - 1, 0), 0)),
                  pl.BlockSpec((None, D, tn), lambda m, n: (l, 0, n)),
                  pl.BlockSpec((None, D, tn), lambda m, n: (l, 0, nf + n)),
                  pl.BlockSpec((CONV_W, tn), lambda m, n: (0, n)),
                  pl.BlockSpec((CONV_W, tn), lambda m, n: (0, nf + n)),
                  pl.BlockSpec((1, tn), lambda m, n: (0, n)),
                  pl.BlockSpec((1, tn), lambda m, n: (0, nf + n))],
        out_specs=pl.BlockSpec((tm, tn), lambda m, n: (m, n)),
        out_shape=jax.ShapeDtypeStruct((M, F), jnp.bfloat16),
        scratch_shapes=[pltpu.VMEM((H + tm, D), jnp.bfloat16)],
        compiler_params=_params("parallel", "arbitrary"),
        name="ffn_up_conv_gate",
    )(h, h, w_up, w_up, conv_w, conv_w, conv_b.reshape(1, 2 * F), conv_b.reshape(1, 2 * F))


def kernel(x, c, w_ada, b_ada, ada_table, norm_mix_pre, norm_mix_post, norm_ffn_pre, norm_ffn_post,
           w_in, attn_sinks, ssm_a_re, ssm_a_im, ssm_log_dt, ssm_b_re, ssm_b_im, ssm_c_re, ssm_c_im,
           ssm_d, ssm_w_glu, w_branch, w_out, ffn_w_up, ffn_conv_w, ffn_conv_b, ffn_w_down):
    B, S, D = x.shape
    depth = w_in.shape[0]
    T = B * S
    bf16 = jnp.bfloat16

    w_in_l = w_in[:1].astype(bf16)
    w_glu_b = ssm_w_glu.astype(bf16)
    w_branch_rows = w_branch.reshape(depth, N_BRANCH * BRANCH_WIDTH, D)

    mod_shared = _ada_proj(c, w_ada, b_ada)
    h = _normmod(x, norm_mix_pre[0], mod_shared, ada_table[0], 0)
    for l in range(depth):
        tab = ada_table[l]
        casts = [(ffn_w_up, l), (ffn_w_down, l), (w_out, l), (w_branch_rows, l)]
        if l + 1 < depth:
            casts.append((w_in, l + 1))
        proj, cast = _matmul(h.reshape(T, D), w_in_l, 0, bf16, 1024, 768, "in_proj", casts=casts)
        w_up_l, w_down_l, w_out_l, w_branch_l = cast[:4]
        w_branch_l = w_branch_l.reshape(1, N_BRANCH, BRANCH_WIDTH, D)
        if l + 1 < depth:
            w_in_l = cast[4]
        y_a = _swa(proj, attn_sinks[l], B, S)
        tables = _ssm_tables(ssm_a_re[l], ssm_a_im[l], ssm_log_dt[l], ssm_b_re[l], ssm_b_im[l],
                             ssm_c_re[l], ssm_c_im[l], B)
        y_s = _ssm(proj, tables, ssm_d[l], w_glu_b, l, B, S).reshape(T, SSM_WIDTH)
        y_c = _stick_breaking(proj, B, S)
        merged = _merge(y_a, y_s, y_c, w_branch_l, 0, proj, D)
        y = _matmul(merged, w_out_l, 0, bf16, 1024, 1024, "out_proj")
        x, h = _postnorm_residual(x, y, norm_mix_post[l], mod_shared, tab, 2,
                                  next_norm=(norm_ffn_pre[l], tab, 3))

        gact = _ffn_up(h.reshape(T, D), w_up_l, 0, ffn_conv_w[l], ffn_conv_b[l], S)
        y = _matmul(gact, w_down_l, 0, bf16, 1024, 512, "ffn_down")
        if l + 1 < depth:
            x, h = _postnorm_residual(x, y, norm_ffn_post[l], mod_shared, tab, 5,
                                      next_norm=(norm_mix_pre[l + 1], ada_table[l + 1], 0))
        else:
            x = _postnorm_residual(x, y, norm_ffn_post[l], mod_shared, tab, 5)
    return x
```
